```python
import math
import jax
import jax.numpy as jnp
from jax import lax
import numpy as np

D_MODEL = 1024
BATCH = 8
SEQ = 4096
DEPTH = 2

GRID_W = 64
CTX_LEN = 256
HEAD_DIM = 64
A_WIDTH = D_MODEL // 4
A_HEADS = A_WIDTH // HEAD_DIM
A_DECAY_LORA = 32
A_ICLR_LORA = 32
A_GATE_LORA = 64
B_WIDTH = D_MODEL // 2
B_V_DIM = 2 * HEAD_DIM
B_HEADS = B_WIDTH // B_V_DIM
Q_BLOCK = 128
ROPE_BASE = 10000.0
C_WIDTH = D_MODEL - A_WIDTH - B_WIDTH
C_HEADS = C_WIDTH // HEAD_DIM
C_CHUNK = 64
MIN_FORGET = 1e-30
PEER_HEADS = 8
PEER_NKEYS = 128
PEER_EXPERTS = PEER_NKEYS * PEER_NKEYS
PEER_QDIM = 256
PEER_TOPK = 16
PEER_TOKEN_BLOCK = 128
DEEPNORM_ALPHA = (2.0 * DEPTH) ** 0.25
DEEPNORM_BETA = (8.0 * DEPTH) ** -0.25
LN_EPS = 1e-5
RWKV_GN_EPS = 64e-5
IN_SPLITS = (A_WIDTH, A_WIDTH, A_WIDTH, A_DECAY_LORA, A_ICLR_LORA, A_GATE_LORA,
             2 * B_HEADS * HEAD_DIM, 2 * B_HEADS * HEAD_DIM, B_HEADS * B_V_DIM,
             C_WIDTH, C_WIDTH, C_WIDTH, C_WIDTH, C_WIDTH)
IN_WIDTH = sum(IN_SPLITS)

kernel_name = 'hybrid_rwkv7_diffattn_hgrn2_peer_dit'


def split_heads(t, n):
    return t.reshape(t.shape[:-1] + (n, t.shape[-1] // n))


def time_major(t):
    return jnp.swapaxes(t, 0, 1).astype(jnp.float32)


def layer_norm(x, w=None, b=None):
    xf = x.astype(jnp.float32)
    xc = xf - xf.mean(-1, keepdims=True)
    y = (xc * lax.rsqrt((xc * xc).mean(-1, keepdims=True) + LN_EPS)).astype(x.dtype)
    if w is not None:
        y = y * w + b
    return y


def rms_norm(x, w):
    xf = x.astype(jnp.float32)
    return (xf * lax.rsqrt((xf * xf).mean(-1, keepdims=True) + LN_EPS)).astype(x.dtype) * w


def modulate(x, shift, scale):
    return layer_norm(x) * (1.0 + scale) + shift


def split_cols(h):
    parts, start = [], 0
    for n in IN_SPLITS:
        parts.append(h[..., start:start + n])
        start += n
    return parts


def short_conv_centered(x, w):
    xp = jnp.pad(x, ((0, 0), (1, 1), (0, 0)))
    return w[0] * xp[:, :-2] + w[1] * xp[:, 1:-1] + w[2] * xp[:, 2:]


def axial_rope_tables(n_rows, dtype):
    row = jnp.repeat(jnp.arange(n_rows), GRID_W).astype(jnp.float32)
    col = jnp.tile(jnp.arange(GRID_W), n_rows).astype(jnp.float32)
    quarter = HEAD_DIM // 4
    inv_freq = ROPE_BASE ** (-2.0 * jnp.arange(quarter, dtype=jnp.float32) / (HEAD_DIM // 2))
    ang_r = row[:, None] * inv_freq
    ang_c = col[:, None] * inv_freq
    shape = (1, row.shape[0], 1, 1, quarter)
    return tuple(f(a).reshape(shape).astype(dtype) for a in (ang_r, ang_c) for f in (jnp.cos, jnp.sin))


def apply_axial_rope(t, rope):
    cr, sr, cc, sc = rope
    x1, x2, x3, x4 = jnp.split(t, 4, axis=-1)
    return jnp.concatenate([x1 * cr - x2 * sr, x2 * cr + x1 * sr,
                            x3 * cc - x4 * sc, x4 * cc + x3 * sc], axis=-1)


def bidirectional_scan(scan_fn, ctx_dirs, lat_dirs, s0):
    rev = lambda tup: tuple(jnp.flip(a, 0) for a in tup)
    s, oc_f = scan_fn(ctx_dirs[0], s0)
    _, ol_f = scan_fn(lat_dirs[0], s)
    s, oc_b = scan_fn(rev(ctx_dirs[1]), s0)
    _, ol_b = scan_fn(rev(lat_dirs[1]), s)
    return oc_f + jnp.flip(oc_b, 0), ol_f + jnp.flip(ol_b, 0)


def rwkv7_scan(inputs, s0):
    def step(S, xs):
        r, w, k, v, a, b = xs
        sa = jnp.einsum('bhvk,bhk->bhv', S, a)
        S = S * w[:, :, None, :] + sa[..., None] * b[:, :, None, :] + v[..., None] * k[:, :, None, :]
        return S, jnp.einsum('bhvk,bhk->bhv', S, r)
    return lax.scan(step, s0, inputs)


def rwkv7_prepare(r, k, v, w_lo, a_lo, g_lo, P):
    r, k, v = jnp.split(short_conv_centered(jnp.concatenate([r, k, v], -1), P['rwkv_conv']), 3, axis=-1)
    a = jax.nn.sigmoid(P['rwkv_a0'] + a_lo @ P['rwkv_a2'])
    g = jax.nn.sigmoid(g_lo) @ P['rwkv_g2']
    kk = split_heads(k * P['rwkv_k_k'], A_HEADS).astype(jnp.float32)
    kk = kk * lax.rsqrt(jnp.maximum(jnp.sum(kk * kk, -1, keepdims=True), 1e-24))
    k = k * (1.0 + (a - 1.0) * P['rwkv_k_a'])
    tw = jnp.tanh(w_lo)
    decays = []
    for d in range(2):
        wl = (P['rwkv_w0'][d] + tw @ P['rwkv_w2'][d]).astype(jnp.float32)
        decays.append(split_heads(jnp.exp(-jnp.exp(-jax.nn.softplus(-wl) - 0.5)), A_HEADS))
    return dict(r=split_heads(r, A_HEADS), k=split_heads(k, A_HEADS), v=split_heads(v, A_HEADS),
                kk=kk, a=split_heads(a, A_HEADS).astype(jnp.float32), decay=tuple(decays), g=g)


def rwkv7_scan_inputs(p):
    r, k, v = time_major(p['r']), time_major(p['k']), time_major(p['v'])
    a_vec = time_major(-p['kk'])
    b_vec = time_major(p['kk'] * p['a'])
    return tuple((r, time_major(dec), k, v, a_vec, b_vec) for dec in p['decay'])


def rwkv7_finish(o, p, P):
    o = jnp.swapaxes(o, 0, 1)
    oc = o - o.mean(-1, keepdims=True)
    y = oc * lax.rsqrt((oc * oc).mean(-1, keepdims=True) + RWKV_GN_EPS)
    bsz, t = y.shape[:2]
    y = y.reshape(bsz, t, A_WIDTH).astype(p['g'].dtype) * P['rwkv_ln_w'] + P['rwkv_ln_b']
    bonus = jnp.sum(p['r'] * p['k'] * split_heads(P['rwkv_r_k'], A_HEADS), -1, keepdims=True) * p['v']
    return (y + bonus.reshape(bsz, t, A_WIDTH)) * p['g']


def diff_softmax_attend(q, k, v, lam):
    s = jnp.einsum('bqhmd,bkhmd->bhmqk', q, k).astype(jnp.float32) * (HEAD_DIM ** -0.5)
    p = jax.nn.softmax(s, axis=-1)
    a = p[:, :, 0] - lam * p[:, :, 1]
    return jnp.einsum('bhqk,bkhe->bqhe', a.astype(v.dtype), v)


def diff_attention_mixer(q_c, k_c, v_c, q_l, k_l, v_l, rope, P, layer_idx, need_ctx):
    qk_heads = lambda t: t.reshape(t.shape[:2] + (B_HEADS, 2, HEAD_DIM))
    q_c, k_c, v_c = qk_heads(q_c), qk_heads(k_c), split_heads(v_c, B_HEADS)
    q_l = apply_axial_rope(qk_heads(q_l), rope)
    k_l = apply_axial_rope(qk_heads(k_l), rope)
    v_l = split_heads(v_l, B_HEADS)
    lam_init = 0.8 - 0.6 * math.exp(-0.3 * layer_idx)
    lq1, lk1, lq2, lk2 = P['diff_lambda'].astype(jnp.float32)
    lam = jnp.exp(jnp.sum(lq1 * lk1)) - jnp.exp(jnp.sum(lq2 * lk2)) + lam_init
    k_all = jnp.concatenate([k_l, k_c], axis=1)
    v_all = jnp.concatenate([v_l, v_c], axis=1)
    bsz, n_lat = q_l.shape[:2]
    n_blk = n_lat // Q_BLOCK
    q_blocks = jnp.moveaxis(q_l.reshape((bsz, n_blk, Q_BLOCK) + q_l.shape[2:]), 1, 0)
    o_blocks = lax.map(lambda qb: diff_softmax_attend(qb, k_all, v_all, lam), q_blocks)
    o_l = jnp.moveaxis(o_blocks, 0, 1).reshape(bsz, n_lat, B_HEADS, B_V_DIM)

    def finish(o):
        y = rms_norm(o, P['diff_subln_w']) * (1.0 - lam_init)
        return y.reshape(y.shape[:2] + (B_WIDTH,))

    y_l = finish(o_l)
    y_c = finish(diff_softmax_attend(q_c, k_c, v_c, lam)) if need_ctx else None
    return y_c, y_l


def hgrn2_chunk_scan(inputs, s0):
    q, k, logf, v = inputs
    n_tok = q.shape[0]
    n_chunk = n_tok // C_CHUNK
    causal = jnp.tril(jnp.ones((C_CHUNK, C_CHUNK), dtype=bool))[:, :, None, None, None]

    def step(S, xs):
        qc, kc, lf, vc = xs
        b = jnp.cumsum(lf, axis=0)
        inter = jnp.einsum('tbhd,bhde->tbhe', qc * jnp.exp(b), S)
        diff = b[:, None] - b[None, :]
        decay = jnp.where(causal, jnp.exp(jnp.where(causal, diff, 0.0)), 0.0)
        attn = jnp.einsum('tbhd,sbhd,tsbhd->bhts', qc, kc, decay)
        intra = jnp.einsum('bhts,sbhe->tbhe', attn, vc)
        b_last = b[-1]
        S = jnp.exp(b_last)[..., None] * S + jnp.einsum('sbhd,sbhe->bhde', kc * jnp.exp(b_last[None] - b), vc)
        return S, inter + intra

    chunked = tuple(a.reshape((n_chunk, C_CHUNK) + a.shape[1:]) for a in inputs)
    S, o = lax.scan(step, s0, chunked)
    return S, o.reshape((n_tok,) + o.shape[2:])


def hgrn2_prepare(q, zf_f, zf_b, i, g, lb):
    ks, logfs = [], []
    for z in (zf_f, zf_b):
        zf = z.astype(jnp.float32)
        f = lb + (1.0 - lb) * jax.nn.sigmoid(zf)
        logfs.append(split_heads(jnp.log(jnp.maximum(f, MIN_FORGET)), C_HEADS))
        ks.append(split_heads((1.0 - lb) * jax.nn.sigmoid(-zf), C_HEADS))
    return dict(q=split_heads(jax.nn.silu(q), C_HEADS), k=tuple(ks), logf=tuple(logfs),
                v=split_heads(i, C_HEADS), g=g)


def hgrn2_scan_inputs(p):
    q, v = time_major(p['q']), time_major(p['v'])
    return tuple((q, time_major(k), time_major(lf), v) for k, lf in zip(p['k'], p['logf']))


def hgrn2_finish(o, p, P):
    o = jnp.swapaxes(o, 0, 1)
    g = split_heads(p['g'], C_HEADS)
    y = rms_norm(o, P['hgrn_norm_w']).astype(g.dtype) * jax.nn.silu(g)
    return y.reshape(y.shape[:2] + (C_WIDTH,))


def token_mixers(h_ctx, h_lat, P, layer_idx, lb, rope, need_ctx):
    pc, pl = split_cols(h_ctx), split_cols(h_lat)
    bsz = h_lat.shape[0]
    ac, al = rwkv7_prepare(*pc[0:6], P), rwkv7_prepare(*pl[0:6], P)
    s0a = jnp.zeros((bsz, A_HEADS, HEAD_DIM, HEAD_DIM), jnp.float32)
    oa_c, oa_l = bidirectional_scan(rwkv7_scan, rwkv7_scan_inputs(ac), rwkv7_scan_inputs(al), s0a)
    ya_l = rwkv7_finish(oa_l, al, P)
    yb_c, yb_l = diff_attention_mixer(*pc[6:9], *pl[6:9], rope, P, layer_idx, need_ctx)
    cc, cl = hgrn2_prepare(*pc[9:14], lb), hgrn2_prepare(*pl[9:14], lb)
    s0c = jnp.zeros((bsz, C_HEADS, HEAD_DIM, HEAD_DIM), jnp.float32)
    oc_c, oc_l = bidirectional_scan(hgrn2_chunk_scan, hgrn2_scan_inputs(cc), hgrn2_scan_inputs(cl), s0c)
    yc_l = hgrn2_finish(oc_l, cl, P)
    y_lat = jnp.concatenate([ya_l, yb_l, yc_l], axis=-1)
    y_ctx = None
    if need_ctx:
        y_ctx = jnp.concatenate([rwkv7_finish(oa_c, ac, P), yb_c, hgrn2_finish(oc_c, cc, P)], axis=-1)
    return y_ctx, y_lat


def peer_ffn(u, P):
    shp = u.shape
    n_tok = shp[0] * shp[1]
    ut = u.reshape(n_tok // PEER_TOKEN_BLOCK, PEER_TOKEN_BLOCK, shp[-1])

    def block(ub):
        t = ub.shape[0]
        q = (ub @ P['peer_wq']).reshape(t, PEER_HEADS, 2, PEER_QDIM // 2)
        s = jnp.einsum('thpd,pkd->thpk', q, P['peer_subkeys']).astype(jnp.float32)
        sv, si = lax.top_k(s, PEER_TOPK)
        cand = (sv[:, :, 0, :, None] + sv[:, :, 1, None, :]).reshape(t, PEER_HEADS, PEER_TOPK * PEER_TOPK)
        top, ci = lax.top_k(cand, PEER_TOPK)
        e = (jnp.take_along_axis(si[:, :, 0], ci // PEER_TOPK, axis=-1) * PEER_NKEYS
             + jnp.take_along_axis(si[:, :, 1], ci % PEER_TOPK, axis=-1))
        gate = jax.nn.softmax(top, axis=-1)
        e = e.reshape(t, PEER_HEADS * PEER_TOPK)
        gate = gate.reshape(t, PEER_HEADS * PEER_TOPK)
        act = jax.nn.gelu(jnp.einsum('tkd,td->tk', P['peer_u'][e], ub).astype(jnp.float32), approximate=False)
        return jnp.einsum('tk,tkd->td', (gate * act).astype(ub.dtype), P['peer_v'][e])

    return lax.map(block, ut).reshape(shp)


def setup_inputs(seed: int = 0) -> dict:
    key = jax.random.key(seed)
    ks = iter(jax.random.split(key, 40))
    f32 = jnp.float32
    nrm = lambda shape, scale=1.0: jax.random.normal(next(ks), shape, f32) * scale
    L, D = DEPTH, D_MODEL
    conv_center = jnp.array([0.0, 1.0, 0.0], f32)[None, :, None]
    return {
        'x': nrm((BATCH, SEQ, D)),
        'c': nrm((BATCH, D)),
        'ctx': nrm((BATCH, CTX_LEN, D)),
        'c_ctx': nrm((D,)),
        'ada_w': nrm((L, D, 6 * D), D ** -0.5),
        'ada_b': nrm((L, 6 * D), 0.01),
        'w_in': nrm((L, D, IN_WIDTH), D ** -0.5),
        'rwkv_conv': conv_center + nrm((L, 3, 3 * A_WIDTH), 0.1),
        'rwkv_w0': jnp.linspace(-6.0, -1.0, A_WIDTH, dtype=f32) + nrm((L, 2, A_WIDTH), 0.1),
        'rwkv_w2': nrm((L, 2, A_DECAY_LORA, A_WIDTH), 0.1),
        'rwkv_a0': nrm((L, A_WIDTH), 0.1),
        'rwkv_a2': nrm((L, A_ICLR_LORA, A_WIDTH), 0.1),
        'rwkv_g2': nrm((L, A_GATE_LORA, A_WIDTH), 2.0 * A_GATE_LORA ** -0.5),
        'rwkv_k_k': 0.85 + nrm((L, A_WIDTH), 0.05),
        'rwkv_k_a': 1.0 + nrm((L, A_WIDTH), 0.05),
        'rwkv_r_k': nrm((L, A_WIDTH), 0.1),
        'rwkv_ln_w': 1.0 + nrm((L, A_WIDTH), 0.05),
        'rwkv_ln_b': nrm((L, A_WIDTH), 0.01),
        'diff_lambda': nrm((L, 4, HEAD_DIM), 0.1),
        'diff_subln_w': 1.0 + nrm((L, B_V_DIM), 0.05),
        'hgrn_lb_logits': 1.0 + nrm((L, C_WIDTH), 0.1),
        'hgrn_norm_w': 1.0 + nrm((L, HEAD_DIM), 0.05),
        'w_out': nrm((L, D, D), DEEPNORM_BETA * D ** -0.5),
        'ln1_w': 1.0 + nrm((L, D), 0.05),
        'ln1_b': nrm((L, D), 0.01),
        'peer_wq': nrm((L, D, PEER_HEADS * PEER_QDIM), D ** -0.5),
        'peer_subkeys': nrm((L, 2, PEER_NKEYS, PEER_QDIM // 2), (PEER_QDIM // 2) ** -0.5),
        'peer_u': nrm((L, PEER_EXPERTS, D), D ** -0.5),
        'peer_v': nrm((L, PEER_EXPERTS, D), DEEPNORM_BETA),
        'ln2_w': 1.0 + nrm((L, D), 0.05),
        'ln2_b': nrm((L, D), 0.01),
    }


def reference(x, c, ctx, c_ctx, ada_w, ada_b, w_in, rwkv_conv, rwkv_w0, rwkv_w2, rwkv_a0, rwkv_a2,
              rwkv_g2, rwkv_k_k, rwkv_k_a, rwkv_r_k, rwkv_ln_w, rwkv_ln_b, diff_lambda, diff_subln_w,
              hgrn_lb_logits, hgrn_norm_w, w_out, ln1_w, ln1_b, peer_wq, peer_subkeys, peer_u, peer_v,
              ln2_w, ln2_b):
    n_rows = x.shape[1] // GRID_W
    rope = axial_rope_tables(n_rows, x.dtype)
    lb_p = jax.nn.softmax(hgrn_lb_logits.astype(jnp.float32), axis=0)
    lower_bounds = jnp.cumsum(lb_p, axis=0) - lb_p[0]
    xc = ctx
    for l in range(DEPTH):
        last = l == DEPTH - 1
        P = dict(rwkv_conv=rwkv_conv[l], rwkv_w0=rwkv_w0[l], rwkv_w2=rwkv_w2[l], rwkv_a0=rwkv_a0[l],
                 rwkv_a2=rwkv_a2[l], rwkv_g2=rwkv_g2[l], rwkv_k_k=rwkv_k_k[l], rwkv_k_a=rwkv_k_a[l],
                 rwkv_r_k=rwkv_r_k[l], rwkv_ln_w=rwkv_ln_w[l], rwkv_ln_b=rwkv_ln_b[l],
                 diff_lambda=diff_lambda[l], diff_subln_w=diff_subln_w[l], hgrn_norm_w=hgrn_norm_w[l],
                 peer_wq=peer_wq[l], peer_subkeys=peer_subkeys[l], peer_u=peer_u[l], peer_v=peer_v[l])
        m_lat = (jax.nn.silu(c) @ ada_w[l] + ada_b[l])[:, None, :]
        m_ctx = jax.nn.silu(c_ctx) @ ada_w[l] + ada_b[l]
        sh1, sc1, g1, sh2, sc2, g2 = jnp.split(m_lat, 6, axis=-1)
        ch1, cs1, cg1, ch2, cs2, cg2 = jnp.split(m_ctx, 6, axis=-1)
        u_lat = modulate(x, sh1, sc1)
        u_ctx = modulate(xc, ch1, cs1)
        y_ctx, y_lat = token_mixers(u_ctx @ w_in[l], u_lat @ w_in[l], P, l, lower_bounds[l], rope, not last)
        x = layer_norm(DEEPNORM_ALPHA * x + g1 * (y_lat @ w_out[l]), ln1_w[l], ln1_b[l])
        x = layer_norm(DEEPNORM_ALPHA * x + g2 * peer_ffn(modulate(x, sh2, sc2), P), ln2_w[l], ln2_b[l])
        if not last:
            xc = layer_norm(DEEPNORM_ALPHA * xc + cg1 * (y_ctx @ w_out[l]), ln1_w[l], ln1_b[l])
            xc = layer_norm(DEEPNORM_ALPHA * xc + cg2 * peer_ffn(modulate(xc, ch2, cs2), P), ln2_w[l], ln2_b[l])
    return x
```

```python
import functools
import math

import numpy as np
import jax
import jax.numpy as jnp
from jax import lax
from jax.experimental import pallas as pl
from jax.experimental.pallas import tpu as pltpu

F32 = jnp.float32
BF16 = jnp.bfloat16
HIGHEST = lax.Precision.HIGHEST

D_MODEL = 1024
GRID_W = 64
HEAD_DIM = 64
A_WIDTH = 256
A_DECAY_LORA = 32
A_ICLR_LORA = 32
A_GATE_LORA = 64
B_WIDTH = 512
B_HEADS = 4
B_V_DIM = 128
C_WIDTH = 256
ROPE_BASE = 10000.0
MIN_FORGET = 1e-30
PEER_HEADS = 8
PEER_NKEYS = 128
PEER_TOPK = 16
PEER_QDIM = 256
LN_EPS = 1e-5
RWKV_GN_EPS = 64e-5

A_COLS = 3 * A_WIDTH + A_DECAY_LORA + A_ICLR_LORA + A_GATE_LORA
QK_COLS = 2 * B_HEADS * HEAD_DIM
C_COLS = 5 * C_WIDTH
IN_WIDTH = A_COLS + 3 * QK_COLS + C_COLS
SCAN_HEADS = 4
CHUNK = 64
STACK = SCAN_HEADS * CHUNK
LANES = 128
VMEM_LIMIT = 56 * 1024 * 1024


def _cparams(n_axes, vmem=None):
    return pltpu.CompilerParams(dimension_semantics=("arbitrary",) * n_axes,
                                vmem_limit_bytes=vmem or VMEM_LIMIT)


def _ln(x):
    xc = x - jnp.mean(x, axis=-1, keepdims=True)
    return xc * lax.rsqrt(jnp.mean(xc * xc, axis=-1, keepdims=True) + LN_EPS)


def _mm(a, b, dims=((1,), (0,)), exact=False):
    dn = (dims, ((), ()))
    if exact:
        return lax.dot_general(a, b, dn, precision=HIGHEST, preferred_element_type=F32)
    return lax.dot_general(a.astype(BF16), b.astype(BF16), dn, preferred_element_type=F32)


NT = ((1,), (1,))
TN = ((0,), (0,))


def _ada_kernel(c_ref, w_ref, b_ref, o_ref):
    c = c_ref[...]
    s = c * jax.nn.sigmoid(c)
    o_ref[...] = _mm(s, w_ref[...], exact=True) + b_ref[...]


def _ada_call(cc, ada_w, ada_b):
    depth, d, n = ada_w.shape
    bn = 512
    return pl.pallas_call(
        _ada_kernel,
        grid=(depth, n // bn),
        in_specs=[pl.BlockSpec(cc.shape, lambda l, j: (0, 0)),
                  pl.BlockSpec((None, d, bn), lambda l, j: (l, 0, j)),
                  pl.BlockSpec((None, 1, bn), lambda l, j: (l, 0, j))],
        out_specs=pl.BlockSpec((None, cc.shape[0], bn), lambda l, j: (l, 0, j)),
        out_shape=jax.ShapeDtypeStruct((depth, cc.shape[0], n), F32),
        compiler_params=_cparams(2),
        name="ada_mod",
    )(cc, ada_w, ada_b.reshape(depth, 1, n))


def _inproj_kernel(x_ref, mod_ref, w_ref, cos_ref, sin_ref, oa_ref, q_ref, k_ref, v_ref, oc_ref):
    m = mod_ref[0, 0]
    u = _ln(x_ref[0]) * (1.0 + m[1:2]) + m[0:1]
    h = _mm(u, w_ref[...])
    oa_ref[0] = h[:, 0:A_COLS]
    cos = cos_ref[...]
    sin = sin_ref[...]
    q0 = A_COLS
    k0 = q0 + QK_COLS
    v0 = k0 + QK_COLS
    c0 = v0 + QK_COLS
    qs0 = IN_WIDTH
    ks0 = qs0 + QK_COLS
    q = (h[:, q0:k0] * cos + h[:, qs0:ks0] * sin) * (HEAD_DIM ** -0.5)
    k = h[:, k0:v0] * cos + h[:, ks0:ks0 + QK_COLS] * sin
    q_ref[0] = q.astype(BF16)
    k_ref[0] = k.astype(BF16)
    v_ref[0] = h[:, v0:c0].astype(BF16)
    oc_ref[0] = h[:, c0:IN_WIDTH]


def _inproj_call(x, mod, w_ext, cos, sin, tm, ctx_blocks):
    b, t, d = x.shape
    n_ext = w_ext.shape[1]
    tok = lambda width: pl.BlockSpec((1, tm, width), lambda i, j: (i, j, 0))
    return pl.pallas_call(
        _inproj_kernel,
        grid=(b, t // tm),
        in_specs=[tok(d),
                  pl.BlockSpec((1, 1, 6, d), lambda i, j: (i, (j >= ctx_blocks).astype(jnp.int32), 0, 0)),
                  pl.BlockSpec((d, n_ext), lambda i, j: (0, 0)),
                  pl.BlockSpec((tm, QK_COLS), lambda i, j: (j, 0)),
                  pl.BlockSpec((tm, QK_COLS), lambda i, j: (j, 0))],
        out_specs=[tok(A_COLS), tok(QK_COLS), tok(QK_COLS), tok(QK_COLS), tok(C_COLS)],
        out_shape=[jax.ShapeDtypeStruct((b, t, A_COLS), F32),
                   jax.ShapeDtypeStruct((b, t, QK_COLS), BF16),
                   jax.ShapeDtypeStruct((b, t, QK_COLS), BF16),
                   jax.ShapeDtypeStruct((b, t, QK_COLS), BF16),
                   jax.ShapeDtypeStruct((b, t, C_COLS), F32)],
        compiler_params=_cparams(2),
        name="in_proj",
    )(x, mod, w_ext, cos, sin)


def _rwkv_prep_kernel(cur_ref, prev_ref, next_ref, conv_ref, w0_ref, w2_ref, a0_ref, a2_ref, g2_ref,
                      kk_ref, ka_ref, rk_ref, hsum_ref,
                      r_o, k_o, v_o, a_o, b_o, lw_o, g_o, bonus_o, *, ctx_blocks, n_blocks):
    j = pl.program_id(1)
    cur = cur_ref[0]
    tm = cur.shape[0]
    w3 = 3 * A_WIDTH
    rkv = cur[:, 0:w3]
    has_prev = jnp.logical_and(j != 0, j != ctx_blocks)
    has_next = jnp.logical_and(j != ctx_blocks - 1, j != n_blocks - 1)
    prev_row = jnp.where(has_prev, prev_ref[0][7:8, 0:w3], 0.0)
    next_row = jnp.where(has_next, next_ref[0][0:1, 0:w3], 0.0)
    row = lax.broadcasted_iota(jnp.int32, (tm, w3), 0)
    xm1 = jnp.where(row == 0, prev_row, pltpu.roll(rkv, 1, 0))
    xp1 = jnp.where(row == tm - 1, next_row, pltpu.roll(rkv, tm - 1, 0))
    cw = conv_ref[...]
    conv = cw[0:1] * xm1 + cw[1:2] * rkv + cw[2:3] * xp1
    r = conv[:, 0:A_WIDTH]
    k = conv[:, A_WIDTH:2 * A_WIDTH]
    v = conv[:, 2 * A_WIDTH:w3]
    o = w3
    w_lo = cur[:, o:o + A_DECAY_LORA]
    a_lo = cur[:, o + A_DECAY_LORA:o + A_DECAY_LORA + A_ICLR_LORA]
    g_lo = cur[:, o + A_DECAY_LORA + A_ICLR_LORA:A_COLS]
    a = jax.nn.sigmoid(a0_ref[...] + _mm(a_lo, a2_ref[...], exact=True))
    g = _mm(jax.nn.sigmoid(g_lo), g2_ref[...], exact=True)
    hsum = hsum_ref[...]
    kk = k * kk_ref[...]
    ss = _mm(kk * kk, hsum, exact=True)
    kk = kk * lax.rsqrt(jnp.maximum(ss, 1e-24))
    k2 = k * (1.0 + (a - 1.0) * ka_ref[...])
    tw = jnp.tanh(w_lo)
    w0 = w0_ref[...]
    for d in range(2):
        wl = w0[d:d + 1] + _mm(tw, w2_ref[d], exact=True)
        lw_o[d, 0] = -math.exp(-0.5) * jax.nn.sigmoid(wl)
    r_o[0] = r
    k_o[0] = k2
    v_o[0] = v
    a_o[0] = -kk
    b_o[0] = kk * a
    g_o[0] = g
    bonus_o[0] = _mm(r * k2 * rk_ref[...], hsum, exact=True) * v


def _rwkv_prep_call(oa, P, hsum, tm, ctx_blocks):
    b, t, _ = oa.shape
    n_blocks = t // tm
    per8 = tm // 8
    last8 = t // 8 - 1
    full = lambda arr: pl.BlockSpec(arr.shape, lambda i, j: (0,) * arr.ndim)
    tok = pl.BlockSpec((1, tm, A_WIDTH), lambda i, j: (i, j, 0))
    row = lambda p: p.reshape(1, -1)
    params = [P['rwkv_conv'], P['rwkv_w0'], P['rwkv_w2'], row(P['rwkv_a0']), P['rwkv_a2'], P['rwkv_g2'],
              row(P['rwkv_k_k']), row(P['rwkv_k_a']), row(P['rwkv_r_k']), hsum]
    shp = jax.ShapeDtypeStruct((b, t, A_WIDTH), F32)
    return pl.pallas_call(
        functools.partial(_rwkv_prep_kernel, ctx_blocks=ctx_blocks, n_blocks=n_blocks),
        grid=(b, n_blocks),
        in_specs=[pl.BlockSpec((1, tm, A_COLS), lambda i, j: (i, j, 0)),
                  pl.BlockSpec((1, 8, A_COLS), lambda i, j: (i, jnp.maximum(j * per8 - 1, 0), 0)),
                  pl.BlockSpec((1, 8, A_COLS), lambda i, j: (i, jnp.minimum((j + 1) * per8, last8), 0))]
                 + [full(p) for p in params],
        out_specs=[tok, tok, tok, tok, tok,
                   pl.BlockSpec((2, 1, tm, A_WIDTH), lambda i, j: (0, i, j, 0)), tok, tok],
        out_shape=[shp, shp, shp, shp, shp, jax.ShapeDtypeStruct((2, b, t, A_WIDTH), F32), shp, shp],
        compiler_params=_cparams(2),
        name="rwkv_prep",
    )(oa, oa, oa, *params)


def _scan_consts():
    c = CHUNK
    t = np.arange(c)
    tri = np.zeros((2, c, c), np.float32)
    tri[0] = (t[None, :] <= t[:, None])
    tri[1] = (t[None, :] >= t[:, None])
    st = np.arange(STACK)
    same_head = (st[:, None] // c) == (st[None, :] // c)
    tt = st[:, None] % c
    ss = st[None, :] % c
    strict = np.stack([same_head & (ss < tt), same_head & (ss > tt)]).astype(np.float32)
    incl = np.stack([same_head & (ss <= tt), same_head & (ss >= tt)]).astype(np.float32)
    head_mask = ((st[:, None] // c) == (np.arange(SCAN_HEADS * HEAD_DIM)[None, :] // HEAD_DIM)).astype(np.float32)
    n_lv = int(math.log2(c))
    seg = np.zeros((2, (n_lv + 1) * c, c), np.float32)
    seg[0, :c] = tri[0]
    seg[1, :c] = tri[1]
    level = -np.ones((2, STACK, STACK), np.int32)
    for d in range(2):
        level[d][same_head & (tt == ss)] = 0
    for l in range(1, n_lv + 1):
        n = 2 ** l
        for ti in range(c):
            s0 = (ti // n) * n
            m = s0 + n // 2 - 1
            hh = s0 + n // 2
            if ti > m:
                seg[0, l * c + ti, m + 1:ti + 1] = 1.0
            else:
                seg[0, l * c + ti, ti + 1:m + 1] = 1.0
            if ti < hh:
                seg[1, l * c + ti, ti:hh] = 1.0
            else:
                seg[1, l * c + ti, hh:ti] = 1.0
        same_blk = (tt // n) == (ss // n)
        t_second = (tt % n) >= n // 2
        s_second = (ss % n) >= n // 2
        level[0][same_head & same_blk & t_second & ~s_second] = l
        level[1][same_head & same_blk & ~t_second & s_second] = l
    return dict(tri=jnp.asarray(tri), strict=jnp.asarray(strict), incl=jnp.asarray(incl),
                head_mask=jnp.asarray(head_mask), seg=jnp.asarray(seg), level=jnp.asarray(level))


def _tile4(x):
    return jnp.concatenate([x] * SCAN_HEADS, axis=0)


def _fold4(x):
    c = CHUNK
    return x[0:c] + x[c:2 * c] + x[2 * c:3 * c] + x[3 * c:4 * c]


def _chunk_index(d, j, nc_ctx, nc):
    back = jnp.where(j < nc_ctx, nc_ctx - 1 - j, nc - 1 - (j - nc_ctx))
    return jnp.where(d == 0, j, back)


def _rwkv_scan_kernel(tri_ref, strict_ref, incl_ref, hm_ref, r_ref, k_ref, v_ref, a_ref, b_ref, lw_ref,
                      o_ref, z_ref, *, exact):
    @pl.when(pl.program_id(2) == 0)
    def _():
        z_ref[...] = jnp.zeros_like(z_ref)

    mm = functools.partial(_mm, exact=exact)
    hm = hm_ref[...]
    strict = strict_ref[0] > 0.0
    incl = incl_ref[0] > 0.0
    lw = lw_ref[0, 0]
    cum = _mm(tri_ref[0], lw, exact=True)
    tot = jnp.sum(lw, axis=0, keepdims=True)
    e_neg = jnp.exp(-cum)
    a_h = a_ref[0] * jnp.exp(cum - lw)
    r_h = r_ref[0] * jnp.exp(cum)
    b_h = b_ref[0] * e_neg
    k_h = k_ref[0] * e_neg
    e_end = jnp.exp(tot - cum)
    b_e = b_ref[0] * e_end
    k_e = k_ref[0] * e_end
    a_sm = _tile4(a_h) * hm
    r_sm = _tile4(r_h) * hm
    v_sm = _tile4(v_ref[0]) * hm
    b_rep = _tile4(b_h)
    k_rep = _tile4(k_h)
    zero = jnp.zeros((STACK, STACK), F32)
    n_ab = jnp.where(strict, mm(a_sm, b_rep, NT), zero)
    n_ak = jnp.where(strict, mm(a_sm, k_rep, NT), zero)
    n_rb = jnp.where(incl, mm(r_sm, b_rep, NT), zero)
    n_rk = jnp.where(incl, mm(r_sm, k_rep, NT), zero)
    rows = lax.broadcasted_iota(jnp.int32, (STACK, STACK), 0)
    cols = lax.broadcasted_iota(jnp.int32, (STACK, STACK), 1)
    inv = jnp.where(rows == cols, 1.0, 0.0) + n_ab
    npow = n_ab
    for _ in range(int(math.log2(CHUNK)) - 1):
        npow = mm(npow, npow)
        inv = inv + mm(inv, npow)
    z = z_ref[...]
    u_sm = mm(inv, mm(a_sm, z, NT) + mm(n_ak, v_sm))
    o_sm = mm(r_sm, z, NT) + mm(n_rb, u_sm) + mm(n_rk, v_sm)
    o_ref[0, 0] = _fold4(o_sm)
    z_ref[...] = z * jnp.exp(tot) + mm(u_sm, _tile4(b_e) * hm, TN) + mm(v_sm, _tile4(k_e) * hm, TN)


def _rwkv_scan_call(r, k, v, a, bvec, lw, consts, nc_ctx, exact):
    b, t, w = r.shape
    nc = t // CHUNK
    cidx = lambda d, i, j: _chunk_index(d, j, nc_ctx, nc)
    tok = pl.BlockSpec((1, CHUNK, w), lambda d, i, j: (i, cidx(d, i, j), 0))
    dsel = lambda n: pl.BlockSpec((1, n, n), lambda d, i, j: (d, 0, 0))
    return pl.pallas_call(
        functools.partial(_rwkv_scan_kernel, exact=exact),
        grid=(2, b, nc),
        in_specs=[dsel(CHUNK), dsel(STACK), dsel(STACK),
                  pl.BlockSpec((STACK, w), lambda d, i, j: (0, 0)),
                  tok, tok, tok, tok, tok,
                  pl.BlockSpec((1, 1, CHUNK, w), lambda d, i, j: (d, i, cidx(d, i, j), 0))],
        out_specs=pl.BlockSpec((1, 1, CHUNK, w), lambda d, i, j: (d, i, cidx(d, i, j), 0)),
        out_shape=jax.ShapeDtypeStruct((2, b, t, w), F32),
        scratch_shapes=[pltpu.VMEM((STACK, STACK), F32)],
        compiler_params=_cparams(3),
        name="rwkv_scan",
    )(consts['tri'], consts['strict'], consts['incl'], consts['head_mask'], r, k, v, a, bvec, lw)


def _hgrn_scan_kernel(seg_ref, level_ref, hm_ref, lb_ref, q_ref, zf_ref, v_ref, o_ref, z_ref, *, exact):
    @pl.when(pl.program_id(2) == 0)
    def _():
        z_ref[...] = jnp.zeros_like(z_ref)

    mm = functools.partial(_mm, exact=exact)
    hm = hm_ref[...]
    lb = lb_ref[...]
    qr = q_ref[0]
    q = qr * jax.nn.sigmoid(qr)
    zf = zf_ref[0]
    f = lb + (1.0 - lb) * jax.nn.sigmoid(zf)
    logf = jnp.log(jnp.maximum(f, MIN_FORGET))
    k = (1.0 - lb) * jax.nn.sigmoid(-zf)
    segs = _mm(seg_ref[0], logf, exact=True)
    cum = segs[0:CHUNK]
    tot = jnp.sum(logf, axis=0, keepdims=True)
    level = level_ref[0]
    k_rep = _tile4(k)
    attn = jnp.where(level == 0, mm(_tile4(q) * hm, k_rep, NT), 0.0)
    for l in range(1, int(math.log2(CHUNK)) + 1):
        e = jnp.exp(segs[l * CHUNK:(l + 1) * CHUNK])
        attn = jnp.where(level == l, mm(_tile4(q * e) * hm, _tile4(k * e), NT), attn)
    v_sm = _tile4(v_ref[0]) * hm
    z = z_ref[...]
    o_sm = mm(_tile4(q * jnp.exp(cum)) * hm, z, NT) + mm(attn, v_sm)
    o_ref[0, 0] = _fold4(o_sm)
    z_ref[...] = z * jnp.exp(tot) + mm(v_sm, _tile4(k * jnp.exp(tot - cum)) * hm, TN)


def _hgrn_scan_call(oc, lb, consts, nc_ctx, exact):
    b, t, _ = oc.shape
    w = C_WIDTH
    nc = t // CHUNK
    cidx = lambda d, i, j: _chunk_index(d, j, nc_ctx, nc)
    n_seg = consts['seg'].shape[1]
    return pl.pallas_call(
        functools.partial(_hgrn_scan_kernel, exact=exact),
        grid=(2, b, nc),
        in_specs=[pl.BlockSpec((1, n_seg, CHUNK), lambda d, i, j: (d, 0, 0)),
                  pl.BlockSpec((1, STACK, STACK), lambda d, i, j: (d, 0, 0)),
                  pl.BlockSpec((STACK, w), lambda d, i, j: (0, 0)),
                  pl.BlockSpec((1, w), lambda d, i, j: (0, 0)),
                  pl.BlockSpec((1, CHUNK, w), lambda d, i, j: (i, cidx(d, i, j), 0)),
                  pl.BlockSpec((1, CHUNK, w), lambda d, i, j: (i, cidx(d, i, j), 1 + d)),
                  pl.BlockSpec((1, CHUNK, w), lambda d, i, j: (i, cidx(d, i, j), 3))],
        out_specs=pl.BlockSpec((1, 1, CHUNK, w), lambda d, i, j: (d, i, cidx(d, i, j), 0)),
        out_shape=jax.ShapeDtypeStruct((2, b, t, w), F32),
        scratch_shapes=[pltpu.VMEM((STACK, STACK), F32)],
        compiler_params=_cparams(3),
        name="hgrn_scan",
    )(consts['seg'], consts['level'], consts['head_mask'], lb.reshape(1, w), oc, oc, oc)


def _attn_kernel(lam_ref, w_ref, q_ref, k_ref, v_ref, o_ref, *, lam_init, ctx_len, ctx_blocks):
    j = pl.program_id(2)
    ll = lam_ref[...]
    lam = (jnp.exp(jnp.sum(ll[0:1] * ll[1:2], axis=-1, keepdims=True))
           - jnp.exp(jnp.sum(ll[2:3] * ll[3:4], axis=-1, keepdims=True)) + lam_init)
    q = q_ref[0]
    first = lax.broadcasted_iota(jnp.int32, q.shape, 1) < HEAD_DIM
    zq = jnp.zeros_like(q)
    q1 = jnp.where(first, q, zq)
    q2 = jnp.where(first, zq, q)

    def attend(k, v):
        def softmax_parts(qm):
            s = lax.dot_general(qm, k, (NT, ((), ())), preferred_element_type=F32)
            p = jnp.exp(s - jnp.max(s, axis=-1, keepdims=True))
            return p, 1.0 / jnp.sum(p, axis=-1, keepdims=True)
        p1, i1 = softmax_parts(q1)
        p2, i2 = softmax_parts(q2)
        a = p1 * i1 - p2 * (lam * i2)
        o = jnp.dot(a.astype(BF16), v, preferred_element_type=F32)
        y = o * lax.rsqrt(jnp.mean(o * o, axis=-1, keepdims=True) + LN_EPS)
        o_ref[0] = y * w_ref[...] * (1.0 - lam_init)

    @pl.when(j < ctx_blocks)
    def _():
        attend(k_ref[0, 0:ctx_len], v_ref[0, 0:ctx_len])

    @pl.when(j >= ctx_blocks)
    def _():
        attend(k_ref[0], v_ref[0])


def _attn_call(q, k, v, diff_lambda, subln_w, lam_init, tq, ctx_len, skip_ctx):
    b, t, _ = q.shape
    ctx_blocks = ctx_len // tq
    j0 = ctx_blocks if skip_ctx else 0
    kv = pl.BlockSpec((1, t, B_V_DIM), lambda i, h, j: (i, 0, h))
    qo = pl.BlockSpec((1, tq, B_V_DIM), lambda i, h, j: (i, j + j0, h))
    return pl.pallas_call(
        functools.partial(_attn_kernel, lam_init=lam_init, ctx_len=ctx_len, ctx_blocks=ctx_blocks - j0),
        grid=(b, B_HEADS, t // tq - j0),
        in_specs=[pl.BlockSpec((4, HEAD_DIM), lambda i, h, j: (0, 0)),
                  pl.BlockSpec((1, B_V_DIM), lambda i, h, j: (0, 0)),
                  qo, kv, kv],
        out_specs=qo,
        out_shape=jax.ShapeDtypeStruct((b, t, B_WIDTH), F32),
        compiler_params=_cparams(3),
        name="diff_attn",
    )(diff_lambda, subln_w.reshape(1, B_V_DIM), q, k, v)


def _outproj_kernel(x_ref, mod_ref, of_ref, ob_ref, g_ref, bonus_ref, yb_ref, cf_ref, cb_ref, cg_ref,
                    hsum_ref, lnw_ref, lnb_ref, hw_ref, wout_ref, l1w_ref, l1b_ref, wq_ref,
                    x1_ref, u2_ref, qp_ref, *, alpha):
    m = mod_ref[0, 0]
    hsum = hsum_ref[...]
    inv = 1.0 / HEAD_DIM
    o = of_ref[0, 0] + ob_ref[0, 0]
    oc = o - _mm(o, hsum, exact=True) * inv
    ya = oc * lax.rsqrt(_mm(oc * oc, hsum, exact=True) * inv + RWKV_GN_EPS)
    ya = (ya * lnw_ref[...] + lnb_ref[...] + bonus_ref[0]) * g_ref[0]
    c = cf_ref[0, 0] + cb_ref[0, 0]
    cg = cg_ref[0]
    yc = c * lax.rsqrt(_mm(c * c, hsum, exact=True) * inv + LN_EPS) * hw_ref[...] * (cg * jax.nn.sigmoid(cg))
    y = jnp.concatenate([ya, yb_ref[0], yc], axis=-1)
    proj = _mm(y, wout_ref[...])
    x1 = _ln(alpha * x_ref[0] + m[2:3] * proj) * l1w_ref[...] + l1b_ref[...]
    x1_ref[0] = x1
    u2 = _ln(x1) * (1.0 + m[4:5]) + m[3:4]
    u2_ref[0] = u2
    qp_ref[0] = _mm(u2, wq_ref[...])


def _outproj_call(x, mod, rw_o, rw_g, rw_bonus, yb, hg_o, oc, hsum, P, alpha, tm, ctx_blocks):
    b, t, d = x.shape
    nq = P['peer_wq_bf16'].shape[1]
    tok = lambda width, col=0: pl.BlockSpec((1, tm, width), lambda i, j: (i, j, col))
    dirtok = lambda dd: pl.BlockSpec((1, 1, tm, A_WIDTH), lambda i, j: (dd, i, j, 0))
    full = lambda arr: pl.BlockSpec(arr.shape, lambda i, j: (0,) * arr.ndim)
    row = lambda p: p.reshape(1, -1)
    params = [hsum, row(P['rwkv_ln_w']), row(P['rwkv_ln_b']), row(jnp.tile(P['hgrn_norm_w'], SCAN_HEADS)),
              P['w_out_bf16'], row(P['ln1_w']), row(P['ln1_b']), P['peer_wq_bf16']]
    return pl.pallas_call(
        functools.partial(_outproj_kernel, alpha=alpha),
        grid=(b, t // tm),
        in_specs=[tok(d),
                  pl.BlockSpec((1, 1, 6, d), lambda i, j: (i, (j >= ctx_blocks).astype(jnp.int32), 0, 0)),
                  dirtok(0), dirtok(1), tok(A_WIDTH), tok(A_WIDTH), tok(B_WIDTH),
                  dirtok(0), dirtok(1), tok(C_WIDTH, 4)] + [full(p) for p in params],
        out_specs=[tok(d), tok(d), tok(nq)],
        out_shape=[jax.ShapeDtypeStruct((b, t, d), F32), jax.ShapeDtypeStruct((b, t, d), F32),
                   jax.ShapeDtypeStruct((b, t, nq), F32)],
        compiler_params=_cparams(2),
        name="out_proj",
    )(x, mod, rw_o, rw_o, rw_g, rw_bonus, yb, hg_o, hg_o, oc, *params)


def _topk_rows(s, payload=None):
    n = s.shape[0]
    iota = lax.broadcasted_iota(jnp.int32, s.shape, 0)
    vals, picks = [], []
    for _ in range(PEER_TOPK):
        m = jnp.max(s, axis=0, keepdims=True)
        idx = jnp.min(jnp.where(s == m, iota, n), axis=0, keepdims=True)
        hit = iota == idx
        vals.append(m)
        picks.append(idx if payload is None else jnp.max(jnp.where(hit, payload, -1), axis=0, keepdims=True))
        s = jnp.where(hit, -jnp.inf, s)
    return jnp.concatenate(vals, axis=0), jnp.concatenate(picks, axis=0)


def _peer_route_kernel(q_ref, keys_ref, e_ref, g_ref):
    half = PEER_QDIM // 2

    def head(h, carry):
        sv, si = [], []
        for p in range(2):
            start = pl.multiple_of((2 * h + p) * half, half)
            qhp = q_ref[:, pl.ds(start, half)]
            s = _mm(keys_ref[p], qhp, NT, exact=True)
            vals, idx = _topk_rows(s)
            sv.append(vals)
            si.append(idx)
        cand = jnp.concatenate([sv[0][i:i + 1] + sv[1] for i in range(PEER_TOPK)], axis=0)
        eid = jnp.concatenate([si[0][i:i + 1] * PEER_NKEYS + si[1] for i in range(PEER_TOPK)], axis=0)
        top, e = _topk_rows(cand, eid)
        ex = jnp.exp(top - top[0:1])
        gate = ex / jnp.sum(ex, axis=0, keepdims=True)
        off = pl.multiple_of(h * PEER_TOPK, PEER_TOPK)
        e_ref[0, pl.ds(off, PEER_TOPK), :] = e
        g_ref[0, pl.ds(off, PEER_TOPK), :] = gate
        return carry

    lax.fori_loop(0, PEER_HEADS, head, 0)


def _peer_route_call(qp, subkeys, tb):
    n, nq = qp.shape
    nblk = n // tb
    nsel = PEER_HEADS * PEER_TOPK
    out = pl.BlockSpec((1, nsel, tb), lambda i: (i, 0, 0))
    return pl.pallas_call(
        _peer_route_kernel,
        grid=(nblk,),
        in_specs=[pl.BlockSpec((tb, nq), lambda i: (i, 0)),
                  pl.BlockSpec(subkeys.shape, lambda i: (0, 0, 0))],
        out_specs=[out, out],
        out_shape=[jax.ShapeDtypeStruct((nblk, nsel, tb), jnp.int32),
                   jax.ShapeDtypeStruct((nblk, nsel, tb), F32)],
        compiler_params=_cparams(1),
        name="peer_route",
    )(qp, subkeys)


PACK_ROWS = 4
NSEL = PEER_HEADS * PEER_TOPK


def _pack_table(tab):
    n, d = tab.shape
    bits = lax.bitcast_convert_type(tab.astype(BF16), jnp.uint16).astype(jnp.uint32)
    word = bits[:, :d // 2] | (bits[:, d // 2:] << 16)
    return lax.bitcast_convert_type(word, jnp.int32).reshape(n * PACK_ROWS, LANES)


def _gather_rows(idx_ref, tab_ref, tile_ref, t):
    base = t * NSEL
    for mi in range(NSEL):
        i = pl.multiple_of(idx_ref[base + mi], PACK_ROWS)
        tile_ref[pl.ds(PACK_ROWS * mi, PACK_ROWS), :] = tab_ref[pl.ds(i, PACK_ROWS), :]
    g = jnp.concatenate([tile_ref[pl.ds(c, NSEL, stride=PACK_ROWS), :] for c in range(PACK_ROWS)], axis=1)
    return pltpu.bitcast(g, BF16)


def _split_hi_lo(x):
    hi = x.astype(BF16).astype(F32)
    return hi, x - hi


def _peer_act_kernel(idx_ref, u_ref, gate_ref, tab_ref, w_ref, tile_ref, act_ref):
    tb = u_ref.shape[0]
    half = D_MODEL // 2
    sub = lax.broadcasted_iota(jnp.int32, (8, half), 0)
    even = (lax.broadcasted_iota(jnp.int32, (1, 2 * NSEL), 1) % 2) == 0

    def token(t, carry):
        gb = _gather_rows(idx_ref, tab_ref, tile_ref, t)
        hi, lo = _split_hi_lo(u_ref[pl.ds(t, 1), :])
        lhs = jnp.where(sub == 0, hi[:, :half], jnp.where(sub == 1, hi[:, half:],
              jnp.where(sub == 2, lo[:, :half], jnp.where(sub == 3, lo[:, half:], 0.0))))
        r = lax.dot_general(lhs.astype(BF16), gb, (NT, ((), ())), preferred_element_type=F32)
        act_ref[pl.ds(t, 1), :] = jnp.where(even, r[0:1] + r[2:3], r[1:2] + r[3:4])
        return carry

    lax.fori_loop(0, tb, token, 0)
    part = act_ref[...]
    lane_even = (lax.broadcasted_iota(jnp.int32, part.shape, 1) % 2) == 0
    act = part + jnp.where(lane_even, pltpu.roll(part, 2 * NSEL - 1, 1), pltpu.roll(part, 1, 1))
    w_ref[...] = gate_ref[...] * (0.5 * act * (1.0 + lax.erf(act * math.sqrt(0.5))))


def _peer_act_call(idx, u2, gate2, tab, tb):
    n, d = u2.shape
    return pl.pallas_call(
        _peer_act_kernel,
        grid=(n // tb,),
        in_specs=[pl.BlockSpec((tb * NSEL,), lambda i: (i,), memory_space=pltpu.SMEM),
                  pl.BlockSpec((tb, d), lambda i: (i, 0)),
                  pl.BlockSpec((tb, 2 * NSEL), lambda i: (i, 0)),
                  pl.BlockSpec(memory_space=pltpu.VMEM)],
        out_specs=pl.BlockSpec((tb, 2 * NSEL), lambda i: (i, 0)),
        out_shape=jax.ShapeDtypeStruct((n, 2 * NSEL), F32),
        scratch_shapes=[pltpu.VMEM((PACK_ROWS * NSEL, LANES), jnp.int32),
                        pltpu.VMEM((tb, 2 * NSEL), F32)],
        compiler_params=_cparams(1),
        name="peer_act",
    )(idx, u2, gate2, tab)


def _peer_out_kernel(idx_ref, w_ref, x1_ref, mod_ref, l2w_ref, l2b_ref, tab_ref, x2_ref, tile_ref, acc_ref,
                     *, alpha):
    tb = w_ref.shape[0]
    half = D_MODEL // 2
    sub = lax.broadcasted_iota(jnp.int32, (8, 2 * NSEL), 0)
    even = (lax.broadcasted_iota(jnp.int32, (8, 2 * NSEL), 1) % 2) == 0

    def token(t, carry):
        gb = _gather_rows(idx_ref, tab_ref, tile_ref, t)
        hi, lo = _split_hi_lo(w_ref[pl.ds(t, 1), :])
        lhs = jnp.where((sub == 0) & even, hi, jnp.where((sub == 1) & ~even, hi,
              jnp.where((sub == 2) & even, lo, jnp.where((sub == 3) & ~even, lo, 0.0))))
        r = jnp.dot(lhs.astype(BF16), gb, preferred_element_type=F32)
        acc_ref[pl.ds(t, 1), 0:half] = r[0:1] + r[2:3]
        acc_ref[pl.ds(t, 1), half:D_MODEL] = r[1:2] + r[3:4]
        return carry

    lax.fori_loop(0, tb, token, 0)
    m = mod_ref[0, 0]
    x2_ref[...] = _ln(alpha * x1_ref[...] + m[5:6] * acc_ref[...]) * l2w_ref[...] + l2b_ref[...]


def _peer_out_call(idx, w2, x1, mod, ln2_w, ln2_b, tab, alpha, tb, t_len, ctx_len):
    n, d = x1.shape
    seg = lambda i: (((i * tb) % t_len) >= ctx_len).astype(jnp.int32)
    return pl.pallas_call(
        functools.partial(_peer_out_kernel, alpha=alpha),
        grid=(n // tb,),
        in_specs=[pl.BlockSpec((tb * NSEL,), lambda i: (i,), memory_space=pltpu.SMEM),
                  pl.BlockSpec((tb, 2 * NSEL), lambda i: (i, 0)),
                  pl.BlockSpec((tb, d), lambda i: (i, 0)),
                  pl.BlockSpec((1, 1, 6, d), lambda i: ((i * tb) // t_len, seg(i), 0, 0)),
                  pl.BlockSpec((1, d), lambda i: (0, 0)),
                  pl.BlockSpec((1, d), lambda i: (0, 0)),
                  pl.BlockSpec(memory_space=pltpu.VMEM)],
        out_specs=pl.BlockSpec((tb, d), lambda i: (i, 0)),
        out_shape=jax.ShapeDtypeStruct((n, d), F32),
        scratch_shapes=[pltpu.VMEM((PACK_ROWS * NSEL, LANES), jnp.int32),
                        pltpu.VMEM((tb, d), F32)],
        compiler_params=_cparams(1),
        name="peer_out",
    )(idx, w2, x1, mod, ln2_w.reshape(1, d), ln2_b.reshape(1, d), tab)


def _rope_tables(n_rows, ctx_len):
    row = jnp.repeat(jnp.arange(n_rows), GRID_W).astype(F32)
    col = jnp.tile(jnp.arange(GRID_W), n_rows).astype(F32)
    quarter = HEAD_DIM // 4
    inv_freq = ROPE_BASE ** (-2.0 * jnp.arange(quarter, dtype=F32) / (HEAD_DIM // 2))
    ang_r = row[:, None] * inv_freq
    ang_c = col[:, None] * inv_freq
    cr, sr, cc, sc = jnp.cos(ang_r), jnp.sin(ang_r), jnp.cos(ang_c), jnp.sin(ang_c)
    cos = jnp.concatenate([cr, cr, cc, cc], axis=-1)
    sin = jnp.concatenate([-sr, sr, -sc, sc], axis=-1)
    cos = jnp.concatenate([jnp.ones((ctx_len, HEAD_DIM), F32), cos], axis=0)
    sin = jnp.concatenate([jnp.zeros((ctx_len, HEAD_DIM), F32), sin], axis=0)
    reps = QK_COLS // HEAD_DIM
    return jnp.tile(cos, (1, reps)), jnp.tile(sin, (1, reps))


def _swap_cols():
    q = HEAD_DIM // 4
    one = np.concatenate([np.arange(q, 2 * q), np.arange(0, q), np.arange(3 * q, 4 * q), np.arange(2 * q, 3 * q)])
    return np.concatenate([one + HEAD_DIM * i for i in range(QK_COLS // HEAD_DIM)])


def kernel(x, c, ctx, c_ctx, ada_w, ada_b, w_in, rwkv_conv, rwkv_w0, rwkv_w2, rwkv_a0, rwkv_a2, rwkv_g2, rwkv_k_k, rwkv_k_a, rwkv_r_k, rwkv_ln_w, rwkv_ln_b, diff_lambda, diff_subln_w, hgrn_lb_logits, hgrn_norm_w, w_out, ln1_w, ln1_b, peer_wq, peer_subkeys, peer_u, peer_v, ln2_w, ln2_b):
    bsz, seq, d = x.shape
    ctx_len = ctx.shape[1]
    depth = w_in.shape[0]
    t_len = ctx_len + seq
    tm = 256 if ctx_len % 256 == 0 else 128
    assert ctx_len % tm == 0 and seq % tm == 0 and seq % GRID_W == 0
    ctx_blocks = ctx_len // tm
    alpha = (2.0 * depth) ** 0.25
    scan_exact = True

    consts = _scan_consts()
    hsum = jnp.asarray((np.arange(A_WIDTH)[:, None] // HEAD_DIM == np.arange(A_WIDTH)[None, :] // HEAD_DIM)
                       .astype(np.float32))
    cos, sin = _rope_tables(seq // GRID_W, ctx_len)
    swap = _swap_cols()

    lb_p = jax.nn.softmax(hgrn_lb_logits.astype(F32), axis=0)
    lower_bounds = jnp.cumsum(lb_p, axis=0) - lb_p[0]

    n_cond = 8 * ((bsz + 1 + 7) // 8)
    cc = jnp.zeros((n_cond, d), F32).at[0].set(c_ctx).at[1:1 + bsz].set(c)
    mods = _ada_call(cc, ada_w, ada_b)

    xs = jnp.concatenate([ctx, x], axis=1)
    for l in range(depth):
        last = l == depth - 1
        m = mods[l].reshape(n_cond, 6, d)
        mod = jnp.stack([jnp.broadcast_to(m[0], (bsz, 6, d)), m[1:1 + bsz]], axis=1)
        q_cols = w_in[l][:, A_COLS:A_COLS + QK_COLS]
        k_cols = w_in[l][:, A_COLS + QK_COLS:A_COLS + 2 * QK_COLS]
        w_ext = jnp.concatenate([w_in[l], q_cols[:, swap], k_cols[:, swap]], axis=1).astype(BF16)
        P = dict(rwkv_conv=rwkv_conv[l], rwkv_w0=rwkv_w0[l], rwkv_w2=rwkv_w2[l], rwkv_a0=rwkv_a0[l],
                 rwkv_a2=rwkv_a2[l], rwkv_g2=rwkv_g2[l], rwkv_k_k=rwkv_k_k[l], rwkv_k_a=rwkv_k_a[l],
                 rwkv_r_k=rwkv_r_k[l], rwkv_ln_w=rwkv_ln_w[l], rwkv_ln_b=rwkv_ln_b[l],
                 hgrn_norm_w=hgrn_norm_w[l], w_out_bf16=w_out[l].astype(BF16),
                 ln1_w=ln1_w[l], ln1_b=ln1_b[l], peer_wq_bf16=peer_wq[l].astype(BF16))

        oa, q, k, v, oc = _inproj_call(xs, mod, w_ext, cos, sin, tm, ctx_blocks)
        r, k2, vv, avec, bvec, lw, g, bonus = _rwkv_prep_call(oa, P, hsum, tm, ctx_blocks)
        rw_o = _rwkv_scan_call(r, k2, vv, avec, bvec, lw, consts, ctx_len // CHUNK, scan_exact)
        hg_o = _hgrn_scan_call(oc, lower_bounds[l], consts, ctx_len // CHUNK, scan_exact)
        lam_init = 0.8 - 0.6 * math.exp(-0.3 * l)
        yb = _attn_call(q, k, v, diff_lambda[l], diff_subln_w[l], lam_init, tm, ctx_len, skip_ctx=False)
        x1, u2, qp = _outproj_call(xs, mod, rw_o, g, bonus, yb, hg_o, oc, hsum, P, alpha, tm, ctx_blocks)

        n_tok = bsz * t_len
        tb_route = 128
        e_t, gate_t = _peer_route_call(qp.reshape(n_tok, -1), peer_subkeys[l], tb_route)
        idx = (jnp.swapaxes(e_t, 1, 2).reshape(n_tok * NSEL) * PACK_ROWS).astype(jnp.int32)
        gate2 = jnp.repeat(jnp.swapaxes(gate_t, 1, 2).reshape(n_tok, NSEL), 2, axis=1)
        tb = 64
        w2 = _peer_act_call(idx, u2.reshape(n_tok, d), gate2, _pack_table(peer_u[l]), tb)
        x2 = _peer_out_call(idx, w2, x1.reshape(n_tok, d), mod, ln2_w[l], ln2_b[l], _pack_table(peer_v[l]),
                            alpha, tb, t_len, ctx_len)
        xs = x2.reshape(bsz, t_len, d)
    return xs[:, ctx_len:]
```

```python
import functools
import math

import numpy as np
import jax
import jax.numpy as jnp
from jax import lax
from jax.experimental import pallas as pl
from jax.experimental.pallas import tpu as pltpu

F32 = jnp.float32
BF16 = jnp.bfloat16
HIGHEST = lax.Precision.HIGHEST

D_MODEL = 1024
GRID_W = 64
HEAD_DIM = 64
A_WIDTH = 256
A_DECAY_LORA = 32
A_ICLR_LORA = 32
A_GATE_LORA = 64
B_WIDTH = 512
B_HEADS = 4
B_V_DIM = 128
C_WIDTH = 256
ROPE_BASE = 10000.0
MIN_FORGET = 1e-30
PEER_HEADS = 8
PEER_NKEYS = 128
PEER_TOPK = 16
PEER_QDIM = 256
LN_EPS = 1e-5
RWKV_GN_EPS = 64e-5

A_COLS = 3 * A_WIDTH + A_DECAY_LORA + A_ICLR_LORA + A_GATE_LORA
QK_COLS = 2 * B_HEADS * HEAD_DIM
C_COLS = 5 * C_WIDTH
IN_WIDTH = A_COLS + 3 * QK_COLS + C_COLS
SCAN_HEADS = 4
CHUNK = 64
STACK = SCAN_HEADS * CHUNK
LANES = 128
VMEM_LIMIT = 56 * 1024 * 1024


def _cparams(n_axes, vmem=None):
    return pltpu.CompilerParams(dimension_semantics=("arbitrary",) * n_axes,
                                vmem_limit_bytes=vmem or VMEM_LIMIT)


def _ln(x):
    xc = x - jnp.mean(x, axis=-1, keepdims=True)
    return xc * lax.rsqrt(jnp.mean(xc * xc, axis=-1, keepdims=True) + LN_EPS)


def _mm(a, b, dims=((1,), (0,)), exact=False):
    dn = (dims, ((), ()))
    if exact is True:
        return lax.dot_general(a, b, dn, precision=HIGHEST, preferred_element_type=F32)
    dot = lambda p, q: lax.dot_general(p, q, dn, preferred_element_type=F32)
    a_hi = a.astype(BF16)
    b_hi = b.astype(BF16)
    if exact == "x3":
        a_lo = (a - a_hi.astype(F32)).astype(BF16)
        b_lo = (b - b_hi.astype(F32)).astype(BF16)
        return dot(a_hi, b_hi) + (dot(a_hi, b_lo) + dot(a_lo, b_hi))
    return dot(a_hi, b_hi)


NT = ((1,), (1,))
TN = ((0,), (0,))


def _ada_kernel(c_ref, w_ref, b_ref, o_ref):
    c = c_ref[...]
    s = c * jax.nn.sigmoid(c)
    o_ref[...] = _mm(s, w_ref[...], exact=True) + b_ref[...]


def _ada_call(cc, ada_w, ada_b):
    depth, d, n = ada_w.shape
    bn = 512
    return pl.pallas_call(
        _ada_kernel,
        grid=(depth, n // bn),
        in_specs=[pl.BlockSpec(cc.shape, lambda l, j: (0, 0)),
                  pl.BlockSpec((None, d, bn), lambda l, j: (l, 0, j)),
                  pl.BlockSpec((None, 1, bn), lambda l, j: (l, 0, j))],
        out_specs=pl.BlockSpec((None, cc.shape[0], bn), lambda l, j: (l, 0, j)),
        out_shape=jax.ShapeDtypeStruct((depth, cc.shape[0], n), F32),
        compiler_params=_cparams(2),
        name="ada_mod",
    )(cc, ada_w, ada_b.reshape(depth, 1, n))


def _inproj_kernel(x_ref, mod_ref, w_ref, cos_ref, sin_ref, oa_ref, q_ref, k_ref, v_ref, oc_ref):
    m = mod_ref[0, 0]
    u = _ln(x_ref[0]) * (1.0 + m[1:2]) + m[0:1]
    h = _mm(u, w_ref[...])
    oa_ref[0] = h[:, 0:A_COLS]
    cos = cos_ref[...]
    sin = sin_ref[...]
    q0 = A_COLS
    k0 = q0 + QK_COLS
    v0 = k0 + QK_COLS
    c0 = v0 + QK_COLS
    qs0 = IN_WIDTH
    ks0 = qs0 + QK_COLS
    q = (h[:, q0:k0] * cos + h[:, qs0:ks0] * sin) * (HEAD_DIM ** -0.5)
    k = h[:, k0:v0] * cos + h[:, ks0:ks0 + QK_COLS] * sin
    q_ref[0] = q.astype(BF16)
    k_ref[0] = k.astype(BF16)
    v_ref[0] = h[:, v0:c0].astype(BF16)
    oc_ref[0] = h[:, c0:IN_WIDTH]


def _inproj_call(x, mod, w_ext, cos, sin, tm, ctx_blocks):
    b, t, d = x.shape
    n_ext = w_ext.shape[1]
    tok = lambda width: pl.BlockSpec((1, tm, width), lambda i, j: (i, j, 0))
    return pl.pallas_call(
        _inproj_kernel,
        grid=(b, t // tm),
        in_specs=[tok(d),
                  pl.BlockSpec((1, 1, 6, d), lambda i, j: (i, (j >= ctx_blocks).astype(jnp.int32), 0, 0)),
                  pl.BlockSpec((d, n_ext), lambda i, j: (0, 0)),
                  pl.BlockSpec((tm, QK_COLS), lambda i, j: (j, 0)),
                  pl.BlockSpec((tm, QK_COLS), lambda i, j: (j, 0))],
        out_specs=[tok(A_COLS), tok(QK_COLS), tok(QK_COLS), tok(QK_COLS), tok(C_COLS)],
        out_shape=[jax.ShapeDtypeStruct((b, t, A_COLS), F32),
                   jax.ShapeDtypeStruct((b, t, QK_COLS), BF16),
                   jax.ShapeDtypeStruct((b, t, QK_COLS), BF16),
                   jax.ShapeDtypeStruct((b, t, QK_COLS), BF16),
                   jax.ShapeDtypeStruct((b, t, C_COLS), F32)],
        compiler_params=_cparams(2),
        name="in_proj",
    )(x, mod, w_ext, cos, sin)


def _rwkv_prep_kernel(cur_ref, prev_ref, next_ref, conv_ref, w0_ref, w2_ref, a0_ref, a2_ref, g2_ref,
                      kk_ref, ka_ref, rk_ref, hsum_ref,
                      r_o, k_o, v_o, a_o, b_o, lw_o, g_o, bonus_o, *, ctx_blocks, n_blocks):
    j = pl.program_id(1)
    cur = cur_ref[0]
    tm = cur.shape[0]
    w3 = 3 * A_WIDTH
    rkv = cur[:, 0:w3]
    has_prev = jnp.logical_and(j != 0, j != ctx_blocks)
    has_next = jnp.logical_and(j != ctx_blocks - 1, j != n_blocks - 1)
    prev_row = jnp.where(has_prev, prev_ref[0][7:8, 0:w3], 0.0)
    next_row = jnp.where(has_next, next_ref[0][0:1, 0:w3], 0.0)
    row = lax.broadcasted_iota(jnp.int32, (tm, w3), 0)
    xm1 = jnp.where(row == 0, prev_row, pltpu.roll(rkv, 1, 0))
    xp1 = jnp.where(row == tm - 1, next_row, pltpu.roll(rkv, tm - 1, 0))
    cw = conv_ref[...]
    conv = cw[0:1] * xm1 + cw[1:2] * rkv + cw[2:3] * xp1
    r = conv[:, 0:A_WIDTH]
    k = conv[:, A_WIDTH:2 * A_WIDTH]
    v = conv[:, 2 * A_WIDTH:w3]
    o = w3
    w_lo = cur[:, o:o + A_DECAY_LORA]
    a_lo = cur[:, o + A_DECAY_LORA:o + A_DECAY_LORA + A_ICLR_LORA]
    g_lo = cur[:, o + A_DECAY_LORA + A_ICLR_LORA:A_COLS]
    a = jax.nn.sigmoid(a0_ref[...] + _mm(a_lo, a2_ref[...], exact=True))
    g = _mm(jax.nn.sigmoid(g_lo), g2_ref[...], exact=True)
    hsum = hsum_ref[...]
    kk = k * kk_ref[...]
    ss = _mm(kk * kk, hsum, exact=True)
    kk = kk * lax.rsqrt(jnp.maximum(ss, 1e-24))
    k2 = k * (1.0 + (a - 1.0) * ka_ref[...])
    tw = jnp.tanh(w_lo)
    w0 = w0_ref[...]
    for d in range(2):
        wl = w0[d:d + 1] + _mm(tw, w2_ref[d], exact=True)
        lw_o[d, 0] = -math.exp(-0.5) * jax.nn.sigmoid(wl)
    r_o[0] = r
    k_o[0] = k2
    v_o[0] = v
    a_o[0] = -kk
    b_o[0] = kk * a
    g_o[0] = g
    bonus_o[0] = _mm(r * k2 * rk_ref[...], hsum, exact=True) * v


def _rwkv_prep_call(oa, P, hsum, tm, ctx_blocks):
    b, t, _ = oa.shape
    n_blocks = t // tm
    per8 = tm // 8
    last8 = t // 8 - 1
    full = lambda arr: pl.BlockSpec(arr.shape, lambda i, j: (0,) * arr.ndim)
    tok = pl.BlockSpec((1, tm, A_WIDTH), lambda i, j: (i, j, 0))
    row = lambda p: p.reshape(1, -1)
    params = [P['rwkv_conv'], P['rwkv_w0'], P['rwkv_w2'], row(P['rwkv_a0']), P['rwkv_a2'], P['rwkv_g2'],
              row(P['rwkv_k_k']), row(P['rwkv_k_a']), row(P['rwkv_r_k']), hsum]
    shp = jax.ShapeDtypeStruct((b, t, A_WIDTH), F32)
    return pl.pallas_call(
        functools.partial(_rwkv_prep_kernel, ctx_blocks=ctx_blocks, n_blocks=n_blocks),
        grid=(b, n_blocks),
        in_specs=[pl.BlockSpec((1, tm, A_COLS), lambda i, j: (i, j, 0)),
                  pl.BlockSpec((1, 8, A_COLS), lambda i, j: (i, jnp.maximum(j * per8 - 1, 0), 0)),
                  pl.BlockSpec((1, 8, A_COLS), lambda i, j: (i, jnp.minimum((j + 1) * per8, last8), 0))]
                 + [full(p) for p in params],
        out_specs=[tok, tok, tok, tok, tok,
                   pl.BlockSpec((2, 1, tm, A_WIDTH), lambda i, j: (0, i, j, 0)), tok, tok],
        out_shape=[shp, shp, shp, shp, shp, jax.ShapeDtypeStruct((2, b, t, A_WIDTH), F32), shp, shp],
        compiler_params=_cparams(2),
        name="rwkv_prep",
    )(oa, oa, oa, *params)


def _scan_consts():
    c = CHUNK
    t = np.arange(c)
    tri = np.zeros((2, c, c), np.float32)
    tri[0] = (t[None, :] <= t[:, None])
    tri[1] = (t[None, :] >= t[:, None])
    st = np.arange(STACK)
    same_head = (st[:, None] // c) == (st[None, :] // c)
    tt = st[:, None] % c
    ss = st[None, :] % c
    strict = np.stack([same_head & (ss < tt), same_head & (ss > tt)]).astype(np.float32)
    incl = np.stack([same_head & (ss <= tt), same_head & (ss >= tt)]).astype(np.float32)
    head_mask = ((st[:, None] // c) == (np.arange(SCAN_HEADS * HEAD_DIM)[None, :] // HEAD_DIM)).astype(np.float32)
    n_lv = int(math.log2(c))
    seg = np.zeros((2, (n_lv + 1) * c, c), np.float32)
    seg[0, :c] = tri[0]
    seg[1, :c] = tri[1]
    level = -np.ones((2, STACK, STACK), np.int32)
    for d in range(2):
        level[d][same_head & (tt == ss)] = 0
    for l in range(1, n_lv + 1):
        n = 2 ** l
        for ti in range(c):
            s0 = (ti // n) * n
            m = s0 + n // 2 - 1
            hh = s0 + n // 2
            if ti > m:
                seg[0, l * c + ti, m + 1:ti + 1] = 1.0
            else:
                seg[0, l * c + ti, ti + 1:m + 1] = 1.0
            if ti < hh:
                seg[1, l * c + ti, ti:hh] = 1.0
            else:
                seg[1, l * c + ti, hh:ti] = 1.0
        same_blk = (tt // n) == (ss // n)
        t_second = (tt % n) >= n // 2
        s_second = (ss % n) >= n // 2
        level[0][same_head & same_blk & t_second & ~s_second] = l
        level[1][same_head & same_blk & ~t_second & s_second] = l
    return dict(tri=jnp.asarray(tri), strict=jnp.asarray(strict), incl=jnp.asarray(incl),
                head_mask=jnp.asarray(head_mask), seg=jnp.asarray(seg), level=jnp.asarray(level))


def _tile4(x):
    return jnp.concatenate([x] * SCAN_HEADS, axis=0)


def _fold4(x):
    c = CHUNK
    return x[0:c] + x[c:2 * c] + x[2 * c:3 * c] + x[3 * c:4 * c]


def _chunk_index(d, j, nc_ctx, nc):
    back = jnp.where(j < nc_ctx, nc_ctx - 1 - j, nc - 1 - (j - nc_ctx))
    return jnp.where(d == 0, j, back)


def _rwkv_scan_kernel(tri_ref, strict_ref, incl_ref, hm_ref, r_ref, k_ref, v_ref, a_ref, b_ref, lw_ref,
                      o_ref, z_ref, *, exact, inv_exact):
    @pl.when(pl.program_id(2) == 0)
    def _():
        z_ref[...] = jnp.zeros_like(z_ref)

    mm = functools.partial(_mm, exact=exact)
    mmi = functools.partial(_mm, exact=inv_exact)
    hm = hm_ref[...]
    strict = strict_ref[0] > 0.0
    incl = incl_ref[0] > 0.0
    lw = lw_ref[0, 0]
    cum = _mm(tri_ref[0], lw, exact=True)
    tot = jnp.sum(lw, axis=0, keepdims=True)
    e_neg = jnp.exp(-cum)
    a_h = a_ref[0] * jnp.exp(cum - lw)
    r_h = r_ref[0] * jnp.exp(cum)
    b_h = b_ref[0] * e_neg
    k_h = k_ref[0] * e_neg
    e_end = jnp.exp(tot - cum)
    b_e = b_ref[0] * e_end
    k_e = k_ref[0] * e_end
    a_sm = _tile4(a_h) * hm
    r_sm = _tile4(r_h) * hm
    v_sm = _tile4(v_ref[0]) * hm
    b_rep = _tile4(b_h)
    k_rep = _tile4(k_h)
    zero = jnp.zeros((STACK, STACK), F32)
    n_ab = jnp.where(strict, mm(a_sm, b_rep, NT), zero)
    n_ak = jnp.where(strict, mm(a_sm, k_rep, NT), zero)
    n_rb = jnp.where(incl, mm(r_sm, b_rep, NT), zero)
    n_rk = jnp.where(incl, mm(r_sm, k_rep, NT), zero)
    rows = lax.broadcasted_iota(jnp.int32, (STACK, STACK), 0)
    cols = lax.broadcasted_iota(jnp.int32, (STACK, STACK), 1)
    inv = jnp.where(rows == cols, 1.0, 0.0) + n_ab
    npow = n_ab
    for _ in range(int(math.log2(CHUNK)) - 1):
        npow = mmi(npow, npow)
        inv = inv + mmi(inv, npow)
    z = z_ref[...]
    u_sm = mm(inv, mm(a_sm, z, NT) + mm(n_ak, v_sm))
    o_sm = mm(r_sm, z, NT) + mm(n_rb, u_sm) + mm(n_rk, v_sm)
    o_ref[0, 0] = _fold4(o_sm)
    z_ref[...] = z * jnp.exp(tot) + mm(u_sm, _tile4(b_e) * hm, TN) + mm(v_sm, _tile4(k_e) * hm, TN)


def _rwkv_scan_call(r, k, v, a, bvec, lw, consts, nc_ctx, exact=False, inv_exact="x3"):
    b, t, w = r.shape
    nc = t // CHUNK
    cidx = lambda d, i, j: _chunk_index(d, j, nc_ctx, nc)
    tok = pl.BlockSpec((1, CHUNK, w), lambda d, i, j: (i, cidx(d, i, j), 0))
    dsel = lambda n: pl.BlockSpec((1, n, n), lambda d, i, j: (d, 0, 0))
    return pl.pallas_call(
        functools.partial(_rwkv_scan_kernel, exact=exact, inv_exact=inv_exact),
        grid=(2, b, nc),
        in_specs=[dsel(CHUNK), dsel(STACK), dsel(STACK),
                  pl.BlockSpec((STACK, w), lambda d, i, j: (0, 0)),
                  tok, tok, tok, tok, tok,
                  pl.BlockSpec((1, 1, CHUNK, w), lambda d, i, j: (d, i, cidx(d, i, j), 0))],
        out_specs=pl.BlockSpec((1, 1, CHUNK, w), lambda d, i, j: (d, i, cidx(d, i, j), 0)),
        out_shape=jax.ShapeDtypeStruct((2, b, t, w), F32),
        scratch_shapes=[pltpu.VMEM((STACK, STACK), F32)],
        compiler_params=_cparams(3),
        name="rwkv_scan",
    )(consts['tri'], consts['strict'], consts['incl'], consts['head_mask'], r, k, v, a, bvec, lw)


def _hgrn_scan_kernel(seg_ref, level_ref, hm_ref, lb_ref, q_ref, zf_ref, v_ref, o_ref, z_ref, *, exact):
    @pl.when(pl.program_id(2) == 0)
    def _():
        z_ref[...] = jnp.zeros_like(z_ref)

    mm = functools.partial(_mm, exact=exact)
    hm = hm_ref[...]
    lb = lb_ref[...]
    qr = q_ref[0]
    q = qr * jax.nn.sigmoid(qr)
    zf = zf_ref[0]
    f = lb + (1.0 - lb) * jax.nn.sigmoid(zf)
    logf = jnp.log(jnp.maximum(f, MIN_FORGET))
    k = (1.0 - lb) * jax.nn.sigmoid(-zf)
    segs = _mm(seg_ref[0], logf, exact=True)
    cum = segs[0:CHUNK]
    tot = jnp.sum(logf, axis=0, keepdims=True)
    level = level_ref[0]
    k_rep = _tile4(k)
    attn = jnp.where(level == 0, mm(_tile4(q) * hm, k_rep, NT), 0.0)
    for l in range(1, int(math.log2(CHUNK)) + 1):
        e = jnp.exp(segs[l * CHUNK:(l + 1) * CHUNK])
        attn = jnp.where(level == l, mm(_tile4(q * e) * hm, _tile4(k * e), NT), attn)
    v_sm = _tile4(v_ref[0]) * hm
    z = z_ref[...]
    o_sm = mm(_tile4(q * jnp.exp(cum)) * hm, z, NT) + mm(attn, v_sm)
    o_ref[0, 0] = _fold4(o_sm)
    z_ref[...] = z * jnp.exp(tot) + mm(v_sm, _tile4(k * jnp.exp(tot - cum)) * hm, TN)


def _hgrn_scan_call(oc, lb, consts, nc_ctx, exact=False):
    b, t, _ = oc.shape
    w = C_WIDTH
    nc = t // CHUNK
    cidx = lambda d, i, j: _chunk_index(d, j, nc_ctx, nc)
    n_seg = consts['seg'].shape[1]
    return pl.pallas_call(
        functools.partial(_hgrn_scan_kernel, exact=exact),
        grid=(2, b, nc),
        in_specs=[pl.BlockSpec((1, n_seg, CHUNK), lambda d, i, j: (d, 0, 0)),
                  pl.BlockSpec((1, STACK, STACK), lambda d, i, j: (d, 0, 0)),
                  pl.BlockSpec((STACK, w), lambda d, i, j: (0, 0)),
                  pl.BlockSpec((1, w), lambda d, i, j: (0, 0)),
                  pl.BlockSpec((1, CHUNK, w), lambda d, i, j: (i, cidx(d, i, j), 0)),
                  pl.BlockSpec((1, CHUNK, w), lambda d, i, j: (i, cidx(d, i, j), 1 + d)),
                  pl.BlockSpec((1, CHUNK, w), lambda d, i, j: (i, cidx(d, i, j), 3))],
        out_specs=pl.BlockSpec((1, 1, CHUNK, w), lambda d, i, j: (d, i, cidx(d, i, j), 0)),
        out_shape=jax.ShapeDtypeStruct((2, b, t, w), F32),
        scratch_shapes=[pltpu.VMEM((STACK, STACK), F32)],
        compiler_params=_cparams(3),
        name="hgrn_scan",
    )(consts['seg'], consts['level'], consts['head_mask'], lb.reshape(1, w), oc, oc, oc)


def _attn_kernel(lam_ref, w_ref, q_ref, k_ref, v_ref, o_ref, *, lam_init, ctx_len, ctx_blocks):
    j = pl.program_id(2)
    ll = lam_ref[...]
    lam = (jnp.exp(jnp.sum(ll[0:1] * ll[1:2], axis=-1, keepdims=True))
           - jnp.exp(jnp.sum(ll[2:3] * ll[3:4], axis=-1, keepdims=True)) + lam_init)
    q = q_ref[0]
    first = lax.broadcasted_iota(jnp.int32, q.shape, 1) < HEAD_DIM
    zq = jnp.zeros_like(q)
    q1 = jnp.where(first, q, zq)
    q2 = jnp.where(first, zq, q)

    def attend(k, v):
        def softmax_parts(qm):
            s = lax.dot_general(qm, k, (NT, ((), ())), preferred_element_type=F32)
            p = jnp.exp(s - jnp.max(s, axis=-1, keepdims=True))
            return p, 1.0 / jnp.sum(p, axis=-1, keepdims=True)
        p1, i1 = softmax_parts(q1)
        p2, i2 = softmax_parts(q2)
        a = p1 * i1 - p2 * (lam * i2)
        o = jnp.dot(a.astype(BF16), v, preferred_element_type=F32)
        y = o * lax.rsqrt(jnp.mean(o * o, axis=-1, keepdims=True) + LN_EPS)
        o_ref[0] = y * w_ref[...] * (1.0 - lam_init)

    @pl.when(j < ctx_blocks)
    def _():
        attend(k_ref[0, 0:ctx_len], v_ref[0, 0:ctx_len])

    @pl.when(j >= ctx_blocks)
    def _():
        attend(k_ref[0], v_ref[0])


def _attn_call(q, k, v, diff_lambda, subln_w, lam_init, tq, ctx_len, skip_ctx):
    b, t, _ = q.shape
    ctx_blocks = ctx_len // tq
    j0 = ctx_blocks if skip_ctx else 0
    kv = pl.BlockSpec((1, t, B_V_DIM), lambda i, h, j: (i, 0, h))
    qo = pl.BlockSpec((1, tq, B_V_DIM), lambda i, h, j: (i, j + j0, h))
    return pl.pallas_call(
        functools.partial(_attn_kernel, lam_init=lam_init, ctx_len=ctx_len, ctx_blocks=ctx_blocks - j0),
        grid=(b, B_HEADS, t // tq - j0),
        in_specs=[pl.BlockSpec((4, HEAD_DIM), lambda i, h, j: (0, 0)),
                  pl.BlockSpec((1, B_V_DIM), lambda i, h, j: (0, 0)),
                  qo, kv, kv],
        out_specs=qo,
        out_shape=jax.ShapeDtypeStruct((b, t, B_WIDTH), F32),
        compiler_params=_cparams(3),
        name="diff_attn",
    )(diff_lambda, subln_w.reshape(1, B_V_DIM), q, k, v)


def _outproj_kernel(x_ref, mod_ref, of_ref, ob_ref, g_ref, bonus_ref, yb_ref, cf_ref, cb_ref, cg_ref,
                    hsum_ref, lnw_ref, lnb_ref, hw_ref, wout_ref, l1w_ref, l1b_ref, wq_ref,
                    x1_ref, u2_ref, qp_ref, *, alpha):
    m = mod_ref[0, 0]
    hsum = hsum_ref[...]
    inv = 1.0 / HEAD_DIM
    o = of_ref[0, 0] + ob_ref[0, 0]
    oc = o - _mm(o, hsum, exact=True) * inv
    ya = oc * lax.rsqrt(_mm(oc * oc, hsum, exact=True) * inv + RWKV_GN_EPS)
    ya = (ya * lnw_ref[...] + lnb_ref[...] + bonus_ref[0]) * g_ref[0]
    c = cf_ref[0, 0] + cb_ref[0, 0]
    cg = cg_ref[0]
    yc = c * lax.rsqrt(_mm(c * c, hsum, exact=True) * inv + LN_EPS) * hw_ref[...] * (cg * jax.nn.sigmoid(cg))
    y = jnp.concatenate([ya, yb_ref[0], yc], axis=-1)
    proj = _mm(y, wout_ref[...])
    x1 = _ln(alpha * x_ref[0] + m[2:3] * proj) * l1w_ref[...] + l1b_ref[...]
    x1_ref[0] = x1
    u2 = _ln(x1) * (1.0 + m[4:5]) + m[3:4]
    u2_ref[0] = u2
    qp_ref[0] = _mm(u2, wq_ref[...])


def _outproj_call(x, mod, rw_o, rw_g, rw_bonus, yb, hg_o, oc, hsum, P, alpha, tm, ctx_blocks, row0):
    b, t, d = x.shape
    nq = P['peer_wq_bf16'].shape[1]
    t_out = t - row0 * tm
    tok = lambda width, col=0: pl.BlockSpec((1, tm, width), lambda i, j: (i, j + row0, col))
    otok = lambda width: pl.BlockSpec((1, tm, width), lambda i, j: (i, j, 0))
    dirtok = lambda dd: pl.BlockSpec((1, 1, tm, A_WIDTH), lambda i, j: (dd, i, j + row0, 0))
    full = lambda arr: pl.BlockSpec(arr.shape, lambda i, j: (0,) * arr.ndim)
    row = lambda p: p.reshape(1, -1)
    params = [hsum, row(P['rwkv_ln_w']), row(P['rwkv_ln_b']), row(jnp.tile(P['hgrn_norm_w'], SCAN_HEADS)),
              P['w_out_bf16'], row(P['ln1_w']), row(P['ln1_b']), P['peer_wq_bf16']]
    return pl.pallas_call(
        functools.partial(_outproj_kernel, alpha=alpha),
        grid=(b, t_out // tm),
        in_specs=[tok(d),
                  pl.BlockSpec((1, 1, 6, d), lambda i, j: (i, (j + row0 >= ctx_blocks).astype(jnp.int32), 0, 0)),
                  dirtok(0), dirtok(1), tok(A_WIDTH), tok(A_WIDTH), tok(B_WIDTH),
                  dirtok(0), dirtok(1), tok(C_WIDTH, 4)] + [full(p) for p in params],
        out_specs=[otok(d), otok(d), otok(nq)],
        out_shape=[jax.ShapeDtypeStruct((b, t_out, d), F32), jax.ShapeDtypeStruct((b, t_out, d), F32),
                   jax.ShapeDtypeStruct((b, t_out, nq), F32)],
        compiler_params=_cparams(2),
        name="out_proj",
    )(x, mod, rw_o, rw_o, rw_g, rw_bonus, yb, hg_o, hg_o, oc, *params)


def _topk_rows(s, order=None, payload=None):
    if order is None:
        order = lax.broadcasted_iota(jnp.int32, s.shape, 0)
    big = jnp.int32(2 ** 30)
    vals, picks = [], []
    for _ in range(PEER_TOPK):
        m = jnp.max(s, axis=0, keepdims=True)
        idx = jnp.min(jnp.where(s == m, order, big), axis=0, keepdims=True)
        hit = order == idx
        vals.append(m)
        picks.append(idx if payload is None else jnp.max(jnp.where(hit, payload, -1), axis=0, keepdims=True))
        s = jnp.where(hit, -jnp.inf, s)
    return jnp.concatenate(vals, axis=0), jnp.concatenate(picks, axis=0)


def _pair_candidates(sv, si):
    k = PEER_TOPK
    vals, flat, eid = [], [], []
    tokens = sv[0].shape[1]
    row8 = lax.broadcasted_iota(jnp.int32, (8, tokens), 0)
    row16 = lax.broadcasted_iota(jnp.int32, (k, tokens), 0)
    neg = -jnp.inf
    for i in range(8):
        n_valid = k // (i + 1)
        rows, row = (k, row16) if n_valid > 8 else (8, row8)
        v = sv[0][i:i + 1] + sv[1][0:rows]
        vals.append(v if n_valid == rows else jnp.where(row < n_valid, v, neg))
        flat.append(i * k + row)
        eid.append(si[0][i:i + 1] * PEER_NKEYS + si[1][0:rows])
    vals.append(sv[0][8:k] + sv[1][0:1])
    flat.append((row8 + 8) * k)
    eid.append(si[0][8:k] * PEER_NKEYS + si[1][0:1])
    cat = lambda xs: jnp.concatenate(xs, axis=0)
    return cat(vals), cat(flat), cat(eid)


def _peer_route_kernel(q_ref, keys_ref, e_ref, g_ref):
    half = PEER_QDIM // 2

    def head(h, carry):
        sv, si = [], []
        for p in range(2):
            start = pl.multiple_of((2 * h + p) * half, half)
            qhp = q_ref[:, pl.ds(start, half)]
            s = _mm(keys_ref[p], qhp, NT, exact=True)
            vals, idx = _topk_rows(s)
            sv.append(vals)
            si.append(idx)
        cand, flat, eid = _pair_candidates(sv, si)
        top, e = _topk_rows(cand, flat, eid)
        ex = jnp.exp(top - top[0:1])
        gate = ex / jnp.sum(ex, axis=0, keepdims=True)
        off = pl.multiple_of(h * PEER_TOPK, PEER_TOPK)
        e_ref[0, pl.ds(off, PEER_TOPK), :] = e
        g_ref[0, pl.ds(off, PEER_TOPK), :] = gate
        return carry

    lax.fori_loop(0, PEER_HEADS, head, 0)


def _peer_route_call(qp, subkeys, tb):
    n, nq = qp.shape
    nblk = n // tb
    nsel = PEER_HEADS * PEER_TOPK
    out = pl.BlockSpec((1, nsel, tb), lambda i: (i, 0, 0))
    return pl.pallas_call(
        _peer_route_kernel,
        grid=(nblk,),
        in_specs=[pl.BlockSpec((tb, nq), lambda i: (i, 0)),
                  pl.BlockSpec(subkeys.shape, lambda i: (0, 0, 0))],
        out_specs=[out, out],
        out_shape=[jax.ShapeDtypeStruct((nblk, nsel, tb), jnp.int32),
                   jax.ShapeDtypeStruct((nblk, nsel, tb), F32)],
        compiler_params=_cparams(1),
        name="peer_route",
    )(qp, subkeys)


PACK_ROWS = 4
NSEL = PEER_HEADS * PEER_TOPK


def _pack_table(tab):
    n, d = tab.shape
    bits = lax.bitcast_convert_type(tab.astype(BF16), jnp.uint16).astype(jnp.uint32)
    word = bits[:, :d // 2] | (bits[:, d // 2:] << 16)
    return lax.bitcast_convert_type(word, jnp.int32).reshape(n * PACK_ROWS, LANES)


TOK_GROUP = 8


def _gather_group(idx_ref, tab_ref, tile_ref, t0):
    tok_idx = [idx_ref.at[pl.ds(pl.multiple_of((t0 + u) * NSEL, NSEL), NSEL)] for u in range(TOK_GROUP)]
    for mi in range(NSEL):
        for u in range(TOK_GROUP):
            i = pl.multiple_of(tok_idx[u][mi], PACK_ROWS)
            tile_ref[u, pl.ds(PACK_ROWS * mi, PACK_ROWS), :] = tab_ref[pl.ds(i, PACK_ROWS), :]


def _packed_rows(tile_ref, u):
    g = jnp.concatenate([tile_ref[u, pl.ds(c, NSEL, stride=PACK_ROWS), :] for c in range(PACK_ROWS)], axis=1)
    return pltpu.bitcast(g, BF16)


def _split_hi_lo(x):
    hi = x.astype(BF16).astype(F32)
    return hi, x - hi


def _peer_act_kernel(idx_ref, u_ref, gate_ref, tab_ref, w_ref, tile_ref, act_ref):
    tb = u_ref.shape[0]
    half = D_MODEL // 2
    sub = lax.broadcasted_iota(jnp.int32, (8, half), 0)
    even = (lax.broadcasted_iota(jnp.int32, (1, 2 * NSEL), 1) % 2) == 0

    def group(gi, carry):
        t0 = gi * TOK_GROUP
        _gather_group(idx_ref, tab_ref, tile_ref, t0)
        for u in range(TOK_GROUP):
            hi, lo = _split_hi_lo(u_ref[pl.ds(t0 + u, 1), :])
            lhs = jnp.where(sub == 0, hi[:, :half], jnp.where(sub == 1, hi[:, half:],
                  jnp.where(sub == 2, lo[:, :half], jnp.where(sub == 3, lo[:, half:], 0.0))))
            r = lax.dot_general(lhs.astype(BF16), _packed_rows(tile_ref, u), (NT, ((), ())),
                                preferred_element_type=F32)
            act_ref[pl.ds(t0 + u, 1), :] = jnp.where(even, r[0:1] + r[2:3], r[1:2] + r[3:4])
        return carry

    lax.fori_loop(0, tb // TOK_GROUP, group, 0)
    part = act_ref[...]
    lane_even = (lax.broadcasted_iota(jnp.int32, part.shape, 1) % 2) == 0
    act = part + jnp.where(lane_even, pltpu.roll(part, 2 * NSEL - 1, 1), pltpu.roll(part, 1, 1))
    w_ref[...] = gate_ref[...] * (0.5 * act * (1.0 + lax.erf(act * math.sqrt(0.5))))


def _peer_act_call(idx, u2, gate2, tab, tb):
    n, d = u2.shape
    return pl.pallas_call(
        _peer_act_kernel,
        grid=(n // tb,),
        in_specs=[pl.BlockSpec((tb * NSEL,), lambda i: (i,), memory_space=pltpu.SMEM),
                  pl.BlockSpec((tb, d), lambda i: (i, 0)),
                  pl.BlockSpec((tb, 2 * NSEL), lambda i: (i, 0)),
                  pl.BlockSpec(memory_space=pltpu.VMEM)],
        out_specs=pl.BlockSpec((tb, 2 * NSEL), lambda i: (i, 0)),
        out_shape=jax.ShapeDtypeStruct((n, 2 * NSEL), F32),
        scratch_shapes=[pltpu.VMEM((TOK_GROUP, PACK_ROWS * NSEL, LANES), jnp.int32),
                        pltpu.VMEM((tb, 2 * NSEL), F32)],
        compiler_params=_cparams(1),
        name="peer_act",
    )(idx, u2, gate2, tab)


def _peer_out_kernel(idx_ref, w_ref, x1_ref, mod_ref, l2w_ref, l2b_ref, tab_ref, x2_ref, tile_ref, acc_ref,
                     *, alpha):
    tb = w_ref.shape[0]
    half = D_MODEL // 2
    sub = lax.broadcasted_iota(jnp.int32, (8, 2 * NSEL), 0)
    even = (lax.broadcasted_iota(jnp.int32, (8, 2 * NSEL), 1) % 2) == 0

    def group(gi, carry):
        t0 = gi * TOK_GROUP
        _gather_group(idx_ref, tab_ref, tile_ref, t0)
        for u in range(TOK_GROUP):
            hi, lo = _split_hi_lo(w_ref[pl.ds(t0 + u, 1), :])
            lhs = jnp.where((sub == 0) & even, hi, jnp.where((sub == 1) & ~even, hi,
                  jnp.where((sub == 2) & even, lo, jnp.where((sub == 3) & ~even, lo, 0.0))))
            r = jnp.dot(lhs.astype(BF16), _packed_rows(tile_ref, u), preferred_element_type=F32)
            acc_ref[pl.ds(t0 + u, 1), 0:half] = r[0:1] + r[2:3]
            acc_ref[pl.ds(t0 + u, 1), half:D_MODEL] = r[1:2] + r[3:4]
        return carry

    lax.fori_loop(0, tb // TOK_GROUP, group, 0)
    m = mod_ref[0, 0]
    x2_ref[...] = _ln(alpha * x1_ref[...] + m[5:6] * acc_ref[...]) * l2w_ref[...] + l2b_ref[...]


def _peer_out_call(idx, w2, x1, mod, ln2_w, ln2_b, tab, alpha, tb, t_len, ctx_len):
    n, d = x1.shape
    seg = lambda i: (((i * tb) % t_len) >= ctx_len).astype(jnp.int32)
    return pl.pallas_call(
        functools.partial(_peer_out_kernel, alpha=alpha),
        grid=(n // tb,),
        in_specs=[pl.BlockSpec((tb * NSEL,), lambda i: (i,), memory_space=pltpu.SMEM),
                  pl.BlockSpec((tb, 2 * NSEL), lambda i: (i, 0)),
                  pl.BlockSpec((tb, d), lambda i: (i, 0)),
                  pl.BlockSpec((1, 1, 6, d), lambda i: ((i * tb) // t_len, seg(i), 0, 0)),
                  pl.BlockSpec((1, d), lambda i: (0, 0)),
                  pl.BlockSpec((1, d), lambda i: (0, 0)),
                  pl.BlockSpec(memory_space=pltpu.VMEM)],
        out_specs=pl.BlockSpec((tb, d), lambda i: (i, 0)),
        out_shape=jax.ShapeDtypeStruct((n, d), F32),
        scratch_shapes=[pltpu.VMEM((TOK_GROUP, PACK_ROWS * NSEL, LANES), jnp.int32),
                        pltpu.VMEM((tb, d), F32)],
        compiler_params=_cparams(1),
        name="peer_out",
    )(idx, w2, x1, mod, ln2_w.reshape(1, d), ln2_b.reshape(1, d), tab)


def _rope_tables(n_rows, ctx_len):
    row = jnp.repeat(jnp.arange(n_rows), GRID_W).astype(F32)
    col = jnp.tile(jnp.arange(GRID_W), n_rows).astype(F32)
    quarter = HEAD_DIM // 4
    inv_freq = ROPE_BASE ** (-2.0 * jnp.arange(quarter, dtype=F32) / (HEAD_DIM // 2))
    ang_r = row[:, None] * inv_freq
    ang_c = col[:, None] * inv_freq
    cr, sr, cc, sc = jnp.cos(ang_r), jnp.sin(ang_r), jnp.cos(ang_c), jnp.sin(ang_c)
    cos = jnp.concatenate([cr, cr, cc, cc], axis=-1)
    sin = jnp.concatenate([-sr, sr, -sc, sc], axis=-1)
    cos = jnp.concatenate([jnp.ones((ctx_len, HEAD_DIM), F32), cos], axis=0)
    sin = jnp.concatenate([jnp.zeros((ctx_len, HEAD_DIM), F32), sin], axis=0)
    reps = QK_COLS // HEAD_DIM
    return jnp.tile(cos, (1, reps)), jnp.tile(sin, (1, reps))


def _swap_cols():
    q = HEAD_DIM // 4
    one = np.concatenate([np.arange(q, 2 * q), np.arange(0, q), np.arange(3 * q, 4 * q), np.arange(2 * q, 3 * q)])
    return np.concatenate([one + HEAD_DIM * i for i in range(QK_COLS // HEAD_DIM)])


def kernel(x, c, ctx, c_ctx, ada_w, ada_b, w_in, rwkv_conv, rwkv_w0, rwkv_w2, rwkv_a0, rwkv_a2, rwkv_g2, rwkv_k_k, rwkv_k_a, rwkv_r_k, rwkv_ln_w, rwkv_ln_b, diff_lambda, diff_subln_w, hgrn_lb_logits, hgrn_norm_w, w_out, ln1_w, ln1_b, peer_wq, peer_subkeys, peer_u, peer_v, ln2_w, ln2_b):
    bsz, seq, d = x.shape
    ctx_len = ctx.shape[1]
    depth = w_in.shape[0]
    t_len = ctx_len + seq
    tm = 256 if ctx_len % 256 == 0 else 128
    assert ctx_len % tm == 0 and seq % tm == 0 and seq % GRID_W == 0
    ctx_blocks = ctx_len // tm
    alpha = (2.0 * depth) ** 0.25

    consts = _scan_consts()
    hsum = jnp.asarray((np.arange(A_WIDTH)[:, None] // HEAD_DIM == np.arange(A_WIDTH)[None, :] // HEAD_DIM)
                       .astype(np.float32))
    cos, sin = _rope_tables(seq // GRID_W, ctx_len)
    swap = _swap_cols()

    lb_p = jax.nn.softmax(hgrn_lb_logits.astype(F32), axis=0)
    lower_bounds = jnp.cumsum(lb_p, axis=0) - lb_p[0]

    n_cond = 8 * ((bsz + 1 + 7) // 8)
    cc = jnp.zeros((n_cond, d), F32).at[0].set(c_ctx).at[1:1 + bsz].set(c)
    mods = _ada_call(cc, ada_w, ada_b)

    xs = jnp.concatenate([ctx, x], axis=1)
    for l in range(depth):
        last = l == depth - 1
        m = mods[l].reshape(n_cond, 6, d)
        mod = jnp.stack([jnp.broadcast_to(m[0], (bsz, 6, d)), m[1:1 + bsz]], axis=1)
        q_cols = w_in[l][:, A_COLS:A_COLS + QK_COLS]
        k_cols = w_in[l][:, A_COLS + QK_COLS:A_COLS + 2 * QK_COLS]
        w_ext = jnp.concatenate([w_in[l], q_cols[:, swap], k_cols[:, swap]], axis=1).astype(BF16)
        P = dict(rwkv_conv=rwkv_conv[l], rwkv_w0=rwkv_w0[l], rwkv_w2=rwkv_w2[l], rwkv_a0=rwkv_a0[l],
                 rwkv_a2=rwkv_a2[l], rwkv_g2=rwkv_g2[l], rwkv_k_k=rwkv_k_k[l], rwkv_k_a=rwkv_k_a[l],
                 rwkv_r_k=rwkv_r_k[l], rwkv_ln_w=rwkv_ln_w[l], rwkv_ln_b=rwkv_ln_b[l],
                 hgrn_norm_w=hgrn_norm_w[l], w_out_bf16=w_out[l].astype(BF16),
                 ln1_w=ln1_w[l], ln1_b=ln1_b[l], peer_wq_bf16=peer_wq[l].astype(BF16))

        oa, q, k, v, oc = _inproj_call(xs, mod, w_ext, cos, sin, tm, ctx_blocks)
        r, k2, vv, avec, bvec, lw, g, bonus = _rwkv_prep_call(oa, P, hsum, tm, ctx_blocks)
        rw_o = _rwkv_scan_call(r, k2, vv, avec, bvec, lw, consts, ctx_len // CHUNK)
        hg_o = _hgrn_scan_call(oc, lower_bounds[l], consts, ctx_len // CHUNK)
        lam_init = 0.8 - 0.6 * math.exp(-0.3 * l)
        row0 = ctx_blocks if last else 0
        t_out = t_len - row0 * tm
        yb = _attn_call(q, k, v, diff_lambda[l], diff_subln_w[l], lam_init, tm, ctx_len, skip_ctx=last)
        x1, u2, qp = _outproj_call(xs, mod, rw_o, g, bonus, yb, hg_o, oc, hsum, P, alpha, tm, ctx_blocks, row0)

        n_tok = bsz * t_out
        tb_route = 128
        e_t, gate_t = _peer_route_call(qp.reshape(n_tok, -1), peer_subkeys[l], tb_route)
        idx = (jnp.swapaxes(e_t, 1, 2).reshape(n_tok * NSEL) * PACK_ROWS).astype(jnp.int32)
        gate2 = jnp.repeat(jnp.swapaxes(gate_t, 1, 2).reshape(n_tok, NSEL), 2, axis=1)
        tb = 64
        w2 = _peer_act_call(idx, u2.reshape(n_tok, d), gate2, _pack_table(peer_u[l]), tb)
        x2 = _peer_out_call(idx, w2, x1.reshape(n_tok, d), mod, ln2_w[l], ln2_b[l], _pack_table(peer_v[l]),
                            alpha, tb, t_out, ctx_len - row0 * tm)
        xs = x2.reshape(bsz, t_out, d)
    return xs
```

```python
import functools
import math

import numpy as np
import jax
import jax.numpy as jnp
from jax import lax
from jax.experimental import pallas as pl
from jax.experimental.pallas import tpu as pltpu

F32 = jnp.float32
BF16 = jnp.bfloat16
HIGHEST = lax.Precision.HIGHEST

D_MODEL = 1024
GRID_W = 64
HEAD_DIM = 64
A_WIDTH = 256
A_DECAY_LORA = 32
A_ICLR_LORA = 32
A_GATE_LORA = 64
B_WIDTH = 512
B_HEADS = 4
B_V_DIM = 128
C_WIDTH = 256
ROPE_BASE = 10000.0
MIN_FORGET = 1e-30
PEER_HEADS = 8
PEER_NKEYS = 128
PEER_TOPK = 16
PEER_QDIM = 256
LN_EPS = 1e-5
RWKV_GN_EPS = 64e-5

A_COLS = 3 * A_WIDTH + A_DECAY_LORA + A_ICLR_LORA + A_GATE_LORA
QK_COLS = 2 * B_HEADS * HEAD_DIM
C_COLS = 5 * C_WIDTH
IN_WIDTH = A_COLS + 3 * QK_COLS + C_COLS
SCAN_HEADS = 4
CHUNK = 64
STACK = SCAN_HEADS * CHUNK
LANES = 128
VMEM_LIMIT = 56 * 1024 * 1024


def _cparams(n_axes, vmem=None):
    return pltpu.CompilerParams(dimension_semantics=("arbitrary",) * n_axes,
                                vmem_limit_bytes=vmem or VMEM_LIMIT)


def _ln(x):
    xc = x - jnp.mean(x, axis=-1, keepdims=True)
    return xc * lax.rsqrt(jnp.mean(xc * xc, axis=-1, keepdims=True) + LN_EPS)


def _mm(a, b, dims=((1,), (0,)), exact=False):
    dn = (dims, ((), ()))
    if exact is True:
        return lax.dot_general(a, b, dn, precision=HIGHEST, preferred_element_type=F32)
    dot = lambda p, q: lax.dot_general(p, q, dn, preferred_element_type=F32)
    a_hi = a.astype(BF16)
    b_hi = b.astype(BF16)
    if exact == "rhs3":
        r1 = b - b_hi.astype(F32)
        b_mid = r1.astype(BF16)
        b_lo = (r1 - b_mid.astype(F32)).astype(BF16)
        return dot(a_hi, b_hi) + (dot(a_hi, b_mid) + dot(a_hi, b_lo))
    if exact == "x3":
        a_lo = (a - a_hi.astype(F32)).astype(BF16)
        b_lo = (b - b_hi.astype(F32)).astype(BF16)
        return dot(a_hi, b_hi) + (dot(a_hi, b_lo) + dot(a_lo, b_hi))
    return dot(a_hi, b_hi)


NT = ((1,), (1,))
TN = ((0,), (0,))


def _ada_kernel(c_ref, w_ref, b_ref, o_ref):
    c = c_ref[...]
    s = c * jax.nn.sigmoid(c)
    o_ref[...] = _mm(s, w_ref[...], exact=True) + b_ref[...]


def _ada_call(cc, ada_w, ada_b):
    depth, d, n = ada_w.shape
    bn = 512
    return pl.pallas_call(
        _ada_kernel,
        grid=(depth, n // bn),
        in_specs=[pl.BlockSpec(cc.shape, lambda l, j: (0, 0)),
                  pl.BlockSpec((None, d, bn), lambda l, j: (l, 0, j)),
                  pl.BlockSpec((None, 1, bn), lambda l, j: (l, 0, j))],
        out_specs=pl.BlockSpec((None, cc.shape[0], bn), lambda l, j: (l, 0, j)),
        out_shape=jax.ShapeDtypeStruct((depth, cc.shape[0], n), F32),
        compiler_params=_cparams(2),
        name="ada_mod",
    )(cc, ada_w, ada_b.reshape(depth, 1, n))


def _inproj_kernel(x_ref, mod_ref, w_ref, cos_ref, sin_ref, oa_ref, q_ref, k_ref, v_ref, oc_ref):
    m = mod_ref[0, 0]
    u = _ln(x_ref[0]) * (1.0 + m[1:2]) + m[0:1]
    h = _mm(u, w_ref[...])
    oa_ref[0] = h[:, 0:A_COLS]
    cos = cos_ref[...]
    sin = sin_ref[...]
    q0 = A_COLS
    k0 = q0 + QK_COLS
    v0 = k0 + QK_COLS
    c0 = v0 + QK_COLS
    qs0 = IN_WIDTH
    ks0 = qs0 + QK_COLS
    q = (h[:, q0:k0] * cos + h[:, qs0:ks0] * sin) * (math.log2(math.e) * HEAD_DIM ** -0.5)
    k = h[:, k0:v0] * cos + h[:, ks0:ks0 + QK_COLS] * sin
    q_ref[0] = q.astype(BF16)
    k_ref[0] = k.astype(BF16)
    v_ref[0] = h[:, v0:c0].astype(BF16)
    oc_ref[0] = h[:, c0:IN_WIDTH]


def _inproj_call(x, mod, w_ext, cos, sin, tm, ctx_blocks):
    b, t, d = x.shape
    n_ext = w_ext.shape[1]
    tok = lambda width: pl.BlockSpec((1, tm, width), lambda i, j: (i, j, 0))
    return pl.pallas_call(
        _inproj_kernel,
        grid=(b, t // tm),
        in_specs=[tok(d),
                  pl.BlockSpec((1, 1, 6, d), lambda i, j: (i, (j >= ctx_blocks).astype(jnp.int32), 0, 0)),
                  pl.BlockSpec((d, n_ext), lambda i, j: (0, 0)),
                  pl.BlockSpec((tm, QK_COLS), lambda i, j: (j, 0)),
                  pl.BlockSpec((tm, QK_COLS), lambda i, j: (j, 0))],
        out_specs=[tok(A_COLS), tok(QK_COLS), tok(QK_COLS), tok(QK_COLS), tok(C_COLS)],
        out_shape=[jax.ShapeDtypeStruct((b, t, A_COLS), F32),
                   jax.ShapeDtypeStruct((b, t, QK_COLS), BF16),
                   jax.ShapeDtypeStruct((b, t, QK_COLS), BF16),
                   jax.ShapeDtypeStruct((b, t, QK_COLS), BF16),
                   jax.ShapeDtypeStruct((b, t, C_COLS), F32)],
        compiler_params=_cparams(2),
        name="in_proj",
    )(x, mod, w_ext, cos, sin)


def _rwkv_prep_kernel(cur_ref, prev_ref, next_ref, conv_ref, w0_ref, w2_ref, a0_ref, a2_ref, g2_ref,
                      kk_ref, ka_ref, rk_ref, hsum_ref,
                      r_o, k_o, v_o, a_o, b_o, lw_o, g_o, bonus_o, *, ctx_blocks, n_blocks):
    j = pl.program_id(1)
    cur = cur_ref[0]
    tm = cur.shape[0]
    w3 = 3 * A_WIDTH
    rkv = cur[:, 0:w3]
    has_prev = jnp.logical_and(j != 0, j != ctx_blocks)
    has_next = jnp.logical_and(j != ctx_blocks - 1, j != n_blocks - 1)
    prev_row = jnp.where(has_prev, prev_ref[0][7:8, 0:w3], 0.0)
    next_row = jnp.where(has_next, next_ref[0][0:1, 0:w3], 0.0)
    row = lax.broadcasted_iota(jnp.int32, (tm, w3), 0)
    xm1 = jnp.where(row == 0, prev_row, pltpu.roll(rkv, 1, 0))
    xp1 = jnp.where(row == tm - 1, next_row, pltpu.roll(rkv, tm - 1, 0))
    cw = conv_ref[...]
    conv = cw[0:1] * xm1 + cw[1:2] * rkv + cw[2:3] * xp1
    r = conv[:, 0:A_WIDTH]
    k = conv[:, A_WIDTH:2 * A_WIDTH]
    v = conv[:, 2 * A_WIDTH:w3]
    o = w3
    w_lo = cur[:, o:o + A_DECAY_LORA]
    a_lo = cur[:, o + A_DECAY_LORA:o + A_DECAY_LORA + A_ICLR_LORA]
    g_lo = cur[:, o + A_DECAY_LORA + A_ICLR_LORA:A_COLS]
    a = jax.nn.sigmoid(a0_ref[...] + _mm(a_lo, a2_ref[...], exact=True))
    g = _mm(jax.nn.sigmoid(g_lo), g2_ref[...], exact=True)
    hsum = hsum_ref[...]
    kk = k * kk_ref[...]
    ss = _mm(kk * kk, hsum, exact=True)
    kk = kk * lax.rsqrt(jnp.maximum(ss, 1e-24))
    k2 = k * (1.0 + (a - 1.0) * ka_ref[...])
    tw = jnp.tanh(w_lo)
    w0 = w0_ref[...]
    for d in range(2):
        wl = w0[d:d + 1] + _mm(tw, w2_ref[d], exact=True)
        lw_o[d, 0] = -math.exp(-0.5) * jax.nn.sigmoid(wl)
    r_o[0] = r
    k_o[0] = k2
    v_o[0] = v
    a_o[0] = -kk
    b_o[0] = kk * a
    g_o[0] = g
    bonus_o[0] = _mm(r * k2 * rk_ref[...], hsum, exact=True) * v


def _rwkv_prep_call(oa, P, hsum, tm, ctx_blocks):
    b, t, _ = oa.shape
    n_blocks = t // tm
    per8 = tm // 8
    last8 = t // 8 - 1
    full = lambda arr: pl.BlockSpec(arr.shape, lambda i, j: (0,) * arr.ndim)
    tok = pl.BlockSpec((1, tm, A_WIDTH), lambda i, j: (i, j, 0))
    row = lambda p: p.reshape(1, -1)
    params = [P['rwkv_conv'], P['rwkv_w0'], P['rwkv_w2'], row(P['rwkv_a0']), P['rwkv_a2'], P['rwkv_g2'],
              row(P['rwkv_k_k']), row(P['rwkv_k_a']), row(P['rwkv_r_k']), hsum]
    shp = jax.ShapeDtypeStruct((b, t, A_WIDTH), F32)
    return pl.pallas_call(
        functools.partial(_rwkv_prep_kernel, ctx_blocks=ctx_blocks, n_blocks=n_blocks),
        grid=(b, n_blocks),
        in_specs=[pl.BlockSpec((1, tm, A_COLS), lambda i, j: (i, j, 0)),
                  pl.BlockSpec((1, 8, A_COLS), lambda i, j: (i, jnp.maximum(j * per8 - 1, 0), 0)),
                  pl.BlockSpec((1, 8, A_COLS), lambda i, j: (i, jnp.minimum((j + 1) * per8, last8), 0))]
                 + [full(p) for p in params],
        out_specs=[tok, tok, tok, tok, tok,
                   pl.BlockSpec((2, 1, tm, A_WIDTH), lambda i, j: (0, i, j, 0)), tok, tok],
        out_shape=[shp, shp, shp, shp, shp, jax.ShapeDtypeStruct((2, b, t, A_WIDTH), F32), shp, shp],
        compiler_params=_cparams(2),
        name="rwkv_prep",
    )(oa, oa, oa, *params)


def _scan_consts():
    c = CHUNK
    t = np.arange(c)
    tri = np.zeros((2, c, c), np.float32)
    tri[0] = (t[None, :] <= t[:, None])
    tri[1] = (t[None, :] >= t[:, None])
    st = np.arange(STACK)
    same_head = (st[:, None] // c) == (st[None, :] // c)
    tt = st[:, None] % c
    ss = st[None, :] % c
    strict = np.stack([same_head & (ss < tt), same_head & (ss > tt)]).astype(np.float32)
    incl = np.stack([same_head & (ss <= tt), same_head & (ss >= tt)]).astype(np.float32)
    head_mask = ((st[:, None] // c) == (np.arange(SCAN_HEADS * HEAD_DIM)[None, :] // HEAD_DIM)).astype(np.float32)
    n_lv = int(math.log2(c))
    seg = np.zeros((2, (n_lv + 1) * c, c), np.float32)
    seg[0, :c] = tri[0]
    seg[1, :c] = tri[1]
    level = -np.ones((2, STACK, STACK), np.int32)
    for d in range(2):
        level[d][same_head & (tt == ss)] = 0
    for l in range(1, n_lv + 1):
        n = 2 ** l
        for ti in range(c):
            s0 = (ti // n) * n
            m = s0 + n // 2 - 1
            hh = s0 + n // 2
            if ti > m:
                seg[0, l * c + ti, m + 1:ti + 1] = 1.0
            else:
                seg[0, l * c + ti, ti + 1:m + 1] = 1.0
            if ti < hh:
                seg[1, l * c + ti, ti:hh] = 1.0
            else:
                seg[1, l * c + ti, hh:ti] = 1.0
        same_blk = (tt // n) == (ss // n)
        t_second = (tt % n) >= n // 2
        s_second = (ss % n) >= n // 2
        level[0][same_head & same_blk & t_second & ~s_second] = l
        level[1][same_head & same_blk & ~t_second & s_second] = l
    return dict(tri=jnp.asarray(tri), strict=jnp.asarray(strict), incl=jnp.asarray(incl),
                head_mask=jnp.asarray(head_mask), seg=jnp.asarray(seg), level=jnp.asarray(level))


def _tile4(x):
    return jnp.concatenate([x] * SCAN_HEADS, axis=0)


def _fold4(x):
    c = CHUNK
    return x[0:c] + x[c:2 * c] + x[2 * c:3 * c] + x[3 * c:4 * c]


def _chunk_index(d, j, nc_ctx, nc):
    if d == 0:
        return j
    return jnp.where(j < nc_ctx, nc_ctx - 1 - j, nc - 1 - (j - nc_ctx))


def _rwkv_chunk(tri, strict, incl, hm, r, k, v, a, b, lw, z, mm, mmi):
    cum = _mm(tri, lw, exact="rhs3")
    tot = jnp.sum(lw, axis=0, keepdims=True)
    e_neg = jnp.exp(-cum)
    e_end = jnp.exp(tot - cum)
    a_sm = _tile4(a * jnp.exp(cum - lw)) * hm
    r_sm = _tile4(r * jnp.exp(cum)) * hm
    v_sm = _tile4(v) * hm
    b_rep = _tile4(b * e_neg)
    k_rep = _tile4(k * e_neg)
    zero = jnp.zeros((STACK, STACK), F32)
    n_ab = jnp.where(strict, mm(a_sm, b_rep, NT), zero)
    n_ak = jnp.where(strict, mm(a_sm, k_rep, NT), zero)
    n_rb = jnp.where(incl, mm(r_sm, b_rep, NT), zero)
    n_rk = jnp.where(incl, mm(r_sm, k_rep, NT), zero)
    rows = lax.broadcasted_iota(jnp.int32, (STACK, STACK), 0)
    cols = lax.broadcasted_iota(jnp.int32, (STACK, STACK), 1)
    inv = jnp.where(rows == cols, 1.0, 0.0) + n_ab
    npow = n_ab
    for _ in range(int(math.log2(CHUNK)) - 1):
        npow = mmi(npow, npow)
        inv = inv + mmi(inv, npow)
    u_sm = mm(inv, mm(a_sm, z, NT) + mm(n_ak, v_sm))
    o_sm = mm(r_sm, z, NT) + mm(n_rb, u_sm) + mm(n_rk, v_sm)
    z_new = z * jnp.exp(tot) + mm(u_sm, _tile4(b * e_end) * hm, TN) + mm(v_sm, _tile4(k * e_end) * hm, TN)
    return _fold4(o_sm), z_new


def _rwkv_scan_kernel(tri_ref, strict_ref, incl_ref, hm_ref, *refs, exact, inv_exact):
    ins, (of_ref, ob_ref, z_ref) = refs[:12], refs[12:]

    @pl.when(pl.program_id(1) == 0)
    def _():
        z_ref[...] = jnp.zeros_like(z_ref)

    mm = functools.partial(_mm, exact=exact)
    mmi = functools.partial(_mm, exact=inv_exact)
    hm = hm_ref[...]
    for d, o_ref in enumerate((of_ref, ob_ref)):
        r, k, v, a, b = (ref[0] for ref in ins[6 * d:6 * d + 5])
        o, z_new = _rwkv_chunk(tri_ref[d], strict_ref[d] > 0.0, incl_ref[d] > 0.0, hm, r, k, v, a, b,
                               ins[6 * d + 5][0, 0], z_ref[d], mm, mmi)
        o_ref[0] = o
        z_ref[d] = z_new


def _rwkv_scan_call(r, k, v, a, bvec, lw, consts, nc_ctx, exact=False, inv_exact=False):
    b, t, w = r.shape
    nc = t // CHUNK
    full = lambda arr: pl.BlockSpec(arr.shape, lambda i, j: (0,) * arr.ndim)
    specs, args = [], []
    for d in range(2):
        cidx = functools.partial(_chunk_index, d, nc_ctx=nc_ctx, nc=nc)
        tok = pl.BlockSpec((1, CHUNK, w), lambda i, j, cidx=cidx: (i, cidx(j), 0))
        specs += [tok] * 5 + [pl.BlockSpec((1, 1, CHUNK, w), lambda i, j, cidx=cidx, d=d: (d, i, cidx(j), 0))]
        args += [r, k, v, a, bvec, lw]
    out_specs = [pl.BlockSpec((1, CHUNK, w), lambda i, j, d=d: (i, _chunk_index(d, j, nc_ctx, nc), 0))
                 for d in range(2)]
    cs = [consts['tri'], consts['strict'], consts['incl'], consts['head_mask']]
    return pl.pallas_call(
        functools.partial(_rwkv_scan_kernel, exact=exact, inv_exact=inv_exact),
        grid=(b, nc),
        in_specs=[full(c) for c in cs] + specs,
        out_specs=out_specs,
        out_shape=[jax.ShapeDtypeStruct((b, t, w), F32)] * 2,
        scratch_shapes=[pltpu.VMEM((2, STACK, STACK), F32)],
        compiler_params=_cparams(2),
        name="rwkv_scan",
    )(*cs, *args)


def _hgrn_chunk(seg, level, hm, lb, qr, zf, v, z, mm):
    q = qr * jax.nn.sigmoid(qr)
    f = lb + (1.0 - lb) * jax.nn.sigmoid(zf)
    logf = jnp.log(jnp.maximum(f, MIN_FORGET))
    k = (1.0 - lb) * jax.nn.sigmoid(-zf)
    segs = _mm(seg, logf, exact="rhs3")
    cum = segs[0:CHUNK]
    tot = jnp.sum(logf, axis=0, keepdims=True)
    attn = jnp.where(level == 0, mm(_tile4(q) * hm, _tile4(k), NT), 0.0)
    for l in range(1, int(math.log2(CHUNK)) + 1):
        e = jnp.exp(segs[l * CHUNK:(l + 1) * CHUNK])
        attn = jnp.where(level == l, mm(_tile4(q * e) * hm, _tile4(k * e), NT), attn)
    v_sm = _tile4(v) * hm
    o_sm = mm(_tile4(q * jnp.exp(cum)) * hm, z, NT) + mm(attn, v_sm)
    z_new = z * jnp.exp(tot) + mm(v_sm, _tile4(k * jnp.exp(tot - cum)) * hm, TN)
    return _fold4(o_sm), z_new


def _hgrn_scan_kernel(seg_ref, level_ref, hm_ref, lb_ref, *refs, exact):
    ins, (of_ref, ob_ref, z_ref) = refs[:6], refs[6:]

    @pl.when(pl.program_id(1) == 0)
    def _():
        z_ref[...] = jnp.zeros_like(z_ref)

    mm = functools.partial(_mm, exact=exact)
    hm = hm_ref[...]
    lb = lb_ref[...]
    for d, o_ref in enumerate((of_ref, ob_ref)):
        qr, zf, v = (ref[0] for ref in ins[3 * d:3 * d + 3])
        o, z_new = _hgrn_chunk(seg_ref[d], level_ref[d], hm, lb, qr, zf, v, z_ref[d], mm)
        o_ref[0] = o
        z_ref[d] = z_new


def _hgrn_scan_call(oc, lb, consts, nc_ctx, exact=False):
    b, t, _ = oc.shape
    w = C_WIDTH
    nc = t // CHUNK
    full = lambda arr: pl.BlockSpec(arr.shape, lambda i, j: (0,) * arr.ndim)
    specs = []
    for d in range(2):
        cidx = functools.partial(_chunk_index, d, nc_ctx=nc_ctx, nc=nc)
        specs += [pl.BlockSpec((1, CHUNK, w), lambda i, j, cidx=cidx, col=col: (i, cidx(j), col))
                  for col in (0, 1 + d, 3)]
    out_specs = [pl.BlockSpec((1, CHUNK, w), lambda i, j, d=d: (i, _chunk_index(d, j, nc_ctx, nc), 0))
                 for d in range(2)]
    cs = [consts['seg'], consts['level'], consts['head_mask'], lb.reshape(1, w)]
    return pl.pallas_call(
        functools.partial(_hgrn_scan_kernel, exact=exact),
        grid=(b, nc),
        in_specs=[full(c) for c in cs] + specs,
        out_specs=out_specs,
        out_shape=[jax.ShapeDtypeStruct((b, t, w), F32)] * 2,
        scratch_shapes=[pltpu.VMEM((2, STACK, STACK), F32)],
        compiler_params=_cparams(2),
        name="hgrn_scan",
    )(*cs, *([oc] * 6))


def _attn_kernel(lam_ref, w_ref, q_ref, k_ref, v_ref, o_ref, *, lam_init, ctx_len, ctx_blocks):
    j = pl.program_id(2)
    ll = lam_ref[...]
    lam = (jnp.exp(jnp.sum(ll[0:1] * ll[1:2], axis=-1, keepdims=True))
           - jnp.exp(jnp.sum(ll[2:3] * ll[3:4], axis=-1, keepdims=True)) + lam_init)
    q = q_ref[0]
    first = lax.broadcasted_iota(jnp.int32, q.shape, 1) < HEAD_DIM
    zq = jnp.zeros_like(q)
    q1 = jnp.where(first, q, zq)
    q2 = jnp.where(first, zq, q)

    def attend(k, v):
        def softmax_parts(qm):
            s = lax.dot_general(qm, k, (NT, ((), ())), preferred_element_type=F32)
            p = jnp.exp2(s - jnp.max(s, axis=-1, keepdims=True))
            return p, 1.0 / jnp.sum(p, axis=-1, keepdims=True)
        p1, i1 = softmax_parts(q1)
        p2, i2 = softmax_parts(q2)
        a = p1 * i1 - p2 * (lam * i2)
        o = jnp.dot(a.astype(BF16), v, preferred_element_type=F32)
        y = o * lax.rsqrt(jnp.mean(o * o, axis=-1, keepdims=True) + LN_EPS)
        o_ref[0] = y * w_ref[...] * (1.0 - lam_init)

    @pl.when(j < ctx_blocks)
    def _():
        attend(k_ref[0, 0:ctx_len], v_ref[0, 0:ctx_len])

    @pl.when(j >= ctx_blocks)
    def _():
        attend(k_ref[0], v_ref[0])


def _attn_call(q, k, v, diff_lambda, subln_w, lam_init, tq, ctx_len, skip_ctx):
    b, t, _ = q.shape
    ctx_blocks = ctx_len // tq
    j0 = ctx_blocks if skip_ctx else 0
    kv = pl.BlockSpec((1, t, B_V_DIM), lambda i, h, j: (i, 0, h))
    qo = pl.BlockSpec((1, tq, B_V_DIM), lambda i, h, j: (i, j + j0, h))
    return pl.pallas_call(
        functools.partial(_attn_kernel, lam_init=lam_init, ctx_len=ctx_len, ctx_blocks=ctx_blocks - j0),
        grid=(b, B_HEADS, t // tq - j0),
        in_specs=[pl.BlockSpec((4, HEAD_DIM), lambda i, h, j: (0, 0)),
                  pl.BlockSpec((1, B_V_DIM), lambda i, h, j: (0, 0)),
                  qo, kv, kv],
        out_specs=qo,
        out_shape=jax.ShapeDtypeStruct((b, t, B_WIDTH), F32),
        compiler_params=_cparams(3),
        name="diff_attn",
    )(diff_lambda, subln_w.reshape(1, B_V_DIM), q, k, v)


def _outproj_kernel(x_ref, mod_ref, of_ref, ob_ref, g_ref, bonus_ref, yb_ref, cf_ref, cb_ref, cg_ref,
                    hsum_ref, lnw_ref, lnb_ref, hw_ref, wout_ref, l1w_ref, l1b_ref, wq_ref,
                    x1_ref, u2_ref, qp_ref, *, alpha):
    m = mod_ref[0, 0]
    hsum = hsum_ref[...]
    inv = 1.0 / HEAD_DIM
    o = of_ref[0] + ob_ref[0]
    oc = o - _mm(o, hsum, exact=True) * inv
    ya = oc * lax.rsqrt(_mm(oc * oc, hsum, exact=True) * inv + RWKV_GN_EPS)
    ya = (ya * lnw_ref[...] + lnb_ref[...] + bonus_ref[0]) * g_ref[0]
    c = cf_ref[0] + cb_ref[0]
    cg = cg_ref[0]
    yc = c * lax.rsqrt(_mm(c * c, hsum, exact=True) * inv + LN_EPS) * hw_ref[...] * (cg * jax.nn.sigmoid(cg))
    y = jnp.concatenate([ya, yb_ref[0], yc], axis=-1)
    proj = _mm(y, wout_ref[...])
    x1 = _ln(alpha * x_ref[0] + m[2:3] * proj) * l1w_ref[...] + l1b_ref[...]
    x1_ref[0] = x1
    u2 = _ln(x1) * (1.0 + m[4:5]) + m[3:4]
    u2_ref[0] = u2
    qp_ref[0] = _mm(u2, wq_ref[...])


def _outproj_call(x, mod, rw_o, rw_g, rw_bonus, yb, hg_o, oc, hsum, P, alpha, tm, ctx_blocks, row0):
    b, t, d = x.shape
    nq = P['peer_wq_bf16'].shape[1]
    t_out = t - row0 * tm
    tok = lambda width, col=0: pl.BlockSpec((1, tm, width), lambda i, j: (i, j + row0, col))
    otok = lambda width: pl.BlockSpec((1, tm, width), lambda i, j: (i, j, 0))
    full = lambda arr: pl.BlockSpec(arr.shape, lambda i, j: (0,) * arr.ndim)
    row = lambda p: p.reshape(1, -1)
    params = [hsum, row(P['rwkv_ln_w']), row(P['rwkv_ln_b']), row(jnp.tile(P['hgrn_norm_w'], SCAN_HEADS)),
              P['w_out_bf16'], row(P['ln1_w']), row(P['ln1_b']), P['peer_wq_bf16']]
    return pl.pallas_call(
        functools.partial(_outproj_kernel, alpha=alpha),
        grid=(b, t_out // tm),
        in_specs=[tok(d),
                  pl.BlockSpec((1, 1, 6, d), lambda i, j: (i, (j + row0 >= ctx_blocks).astype(jnp.int32), 0, 0)),
                  tok(A_WIDTH), tok(A_WIDTH), tok(A_WIDTH), tok(A_WIDTH), tok(B_WIDTH),
                  tok(C_WIDTH), tok(C_WIDTH), tok(C_WIDTH, 4)] + [full(p) for p in params],
        out_specs=[otok(d), otok(d), otok(nq)],
        out_shape=[jax.ShapeDtypeStruct((b, t_out, d), F32), jax.ShapeDtypeStruct((b, t_out, d), F32),
                   jax.ShapeDtypeStruct((b, t_out, nq), F32)],
        compiler_params=_cparams(2),
        name="out_proj",
    )(x, mod, rw_o[0], rw_o[1], rw_g, rw_bonus, yb, hg_o[0], hg_o[1], oc, *params)


def _topk_rows(s, order=None, payload=None):
    if order is None:
        order = lax.broadcasted_iota(jnp.int32, s.shape, 0)
    big = jnp.int32(2 ** 30)
    vals, picks = [], []
    for _ in range(PEER_TOPK):
        m = jnp.max(s, axis=0, keepdims=True)
        idx = jnp.min(jnp.where(s == m, order, big), axis=0, keepdims=True)
        hit = order == idx
        vals.append(m)
        picks.append(idx if payload is None else jnp.max(jnp.where(hit, payload, -1), axis=0, keepdims=True))
        s = jnp.where(hit, -jnp.inf, s)
    return jnp.concatenate(vals, axis=0), jnp.concatenate(picks, axis=0)


def _pair_candidates(sv, si):
    k = PEER_TOPK
    vals, flat, eid = [], [], []
    tokens = sv[0].shape[1]
    row8 = lax.broadcasted_iota(jnp.int32, (8, tokens), 0)
    row16 = lax.broadcasted_iota(jnp.int32, (k, tokens), 0)
    neg = -jnp.inf
    for i in range(8):
        n_valid = k // (i + 1)
        rows, row = (k, row16) if n_valid > 8 else (8, row8)
        v = sv[0][i:i + 1] + sv[1][0:rows]
        vals.append(v if n_valid == rows else jnp.where(row < n_valid, v, neg))
        flat.append(i * k + row)
        eid.append(si[0][i:i + 1] * PEER_NKEYS + si[1][0:rows])
    vals.append(sv[0][8:k] + sv[1][0:1])
    flat.append((row8 + 8) * k)
    eid.append(si[0][8:k] * PEER_NKEYS + si[1][0:1])
    cat = lambda xs: jnp.concatenate(xs, axis=0)
    return cat(vals), cat(flat), cat(eid)


def _peer_route_kernel(q_ref, keys_ref, e_ref, g_ref):
    half = PEER_QDIM // 2

    def head(h, carry):
        sv, si = [], []
        for p in range(2):
            start = pl.multiple_of((2 * h + p) * half, half)
            qhp = q_ref[:, pl.ds(start, half)]
            s = _mm(keys_ref[p], qhp, NT, exact=True)
            vals, idx = _topk_rows(s)
            sv.append(vals)
            si.append(idx)
        cand, flat, eid = _pair_candidates(sv, si)
        top, e = _topk_rows(cand, flat, eid)
        ex = jnp.exp(top - top[0:1])
        gate = ex / jnp.sum(ex, axis=0, keepdims=True)
        off = pl.multiple_of(h * PEER_TOPK, PEER_TOPK)
        e_ref[0, pl.ds(off, PEER_TOPK), :] = e
        g_ref[0, pl.ds(off, PEER_TOPK), :] = gate
        return carry

    lax.fori_loop(0, PEER_HEADS, head, 0)


def _peer_route_call(qp, subkeys, tb):
    n, nq = qp.shape
    nblk = n // tb
    nsel = PEER_HEADS * PEER_TOPK
    out = pl.BlockSpec((1, nsel, tb), lambda i: (i, 0, 0))
    return pl.pallas_call(
        _peer_route_kernel,
        grid=(nblk,),
        in_specs=[pl.BlockSpec((tb, nq), lambda i: (i, 0)),
                  pl.BlockSpec(subkeys.shape, lambda i: (0, 0, 0))],
        out_specs=[out, out],
        out_shape=[jax.ShapeDtypeStruct((nblk, nsel, tb), jnp.int32),
                   jax.ShapeDtypeStruct((nblk, nsel, tb), F32)],
        compiler_params=_cparams(1),
        name="peer_route",
    )(qp, subkeys)


PACK_ROWS = 4
NSEL = PEER_HEADS * PEER_TOPK


def _pack_table(tab):
    n, d = tab.shape
    bits = lax.bitcast_convert_type(tab.astype(BF16), jnp.uint16).astype(jnp.uint32)
    word = bits[:, :d // 2] | (bits[:, d // 2:] << 16)
    return lax.bitcast_convert_type(word, jnp.int32).reshape(n * PACK_ROWS, LANES)


TOK_GROUP = 8


def _gather_group(idx_ref, tab_ref, tile_ref, t0):
    tok_idx = [idx_ref.at[pl.ds(pl.multiple_of((t0 + u) * NSEL, NSEL), NSEL)] for u in range(TOK_GROUP)]
    for mi in range(NSEL):
        for u in range(TOK_GROUP):
            i = pl.multiple_of(tok_idx[u][mi], PACK_ROWS)
            tile_ref[u, pl.ds(PACK_ROWS * mi, PACK_ROWS), :] = tab_ref[pl.ds(i, PACK_ROWS), :]


def _packed_rows(tile_ref, u):
    g = jnp.concatenate([tile_ref[u, pl.ds(c, NSEL, stride=PACK_ROWS), :] for c in range(PACK_ROWS)], axis=1)
    return pltpu.bitcast(g, BF16)


def _split_hi_lo(x):
    hi = x.astype(BF16).astype(F32)
    return hi, x - hi


def _peer_act_kernel(idx_ref, u_ref, gate_ref, tab_ref, w_ref, tile_ref, act_ref):
    tb = u_ref.shape[0]
    half = D_MODEL // 2
    sub = lax.broadcasted_iota(jnp.int32, (8, half), 0)
    even = (lax.broadcasted_iota(jnp.int32, (1, 2 * NSEL), 1) % 2) == 0

    def group(gi, carry):
        t0 = gi * TOK_GROUP
        _gather_group(idx_ref, tab_ref, tile_ref, t0)
        for u in range(TOK_GROUP):
            hi, lo = _split_hi_lo(u_ref[pl.ds(t0 + u, 1), :])
            lhs = jnp.where(sub == 0, hi[:, :half], jnp.where(sub == 1, hi[:, half:],
                  jnp.where(sub == 2, lo[:, :half], jnp.where(sub == 3, lo[:, half:], 0.0))))
            r = lax.dot_general(lhs.astype(BF16), _packed_rows(tile_ref, u), (NT, ((), ())),
                                preferred_element_type=F32)
            act_ref[pl.ds(t0 + u, 1), :] = jnp.where(even, r[0:1] + r[2:3], r[1:2] + r[3:4])
        return carry

    lax.fori_loop(0, tb // TOK_GROUP, group, 0)
    part = act_ref[...]
    lane_even = (lax.broadcasted_iota(jnp.int32, part.shape, 1) % 2) == 0
    act = part + jnp.where(lane_even, pltpu.roll(part, 2 * NSEL - 1, 1), pltpu.roll(part, 1, 1))
    w_ref[...] = gate_ref[...] * (0.5 * act * (1.0 + lax.erf(act * math.sqrt(0.5))))


def _peer_act_call(idx, u2, gate2, tab, tb):
    n, d = u2.shape
    return pl.pallas_call(
        _peer_act_kernel,
        grid=(n // tb,),
        in_specs=[pl.BlockSpec((tb * NSEL,), lambda i: (i,), memory_space=pltpu.SMEM),
                  pl.BlockSpec((tb, d), lambda i: (i, 0)),
                  pl.BlockSpec((tb, 2 * NSEL), lambda i: (i, 0)),
                  pl.BlockSpec(memory_space=pltpu.VMEM)],
        out_specs=pl.BlockSpec((tb, 2 * NSEL), lambda i: (i, 0)),
        out_shape=jax.ShapeDtypeStruct((n, 2 * NSEL), F32),
        scratch_shapes=[pltpu.VMEM((TOK_GROUP, PACK_ROWS * NSEL, LANES), jnp.int32),
                        pltpu.VMEM((tb, 2 * NSEL), F32)],
        compiler_params=_cparams(1),
        name="peer_act",
    )(idx, u2, gate2, tab)


def _peer_out_kernel(idx_ref, w_ref, x1_ref, mod_ref, l2w_ref, l2b_ref, tab_ref, x2_ref, tile_ref, acc_ref,
                     *, alpha):
    tb = w_ref.shape[0]
    half = D_MODEL // 2
    sub = lax.broadcasted_iota(jnp.int32, (8, 2 * NSEL), 0)
    even = (lax.broadcasted_iota(jnp.int32, (8, 2 * NSEL), 1) % 2) == 0

    def group(gi, carry):
        t0 = gi * TOK_GROUP
        _gather_group(idx_ref, tab_ref, tile_ref, t0)
        for u in range(TOK_GROUP):
            hi, lo = _split_hi_lo(w_ref[pl.ds(t0 + u, 1), :])
            lhs = jnp.where((sub == 0) & even, hi, jnp.where((sub == 1) & ~even, hi,
                  jnp.where((sub == 2) & even, lo, jnp.where((sub == 3) & ~even, lo, 0.0))))
            r = jnp.dot(lhs.astype(BF16), _packed_rows(tile_ref, u), preferred_element_type=F32)
            acc_ref[pl.ds(t0 + u, 1), 0:half] = r[0:1] + r[2:3]
            acc_ref[pl.ds(t0 + u, 1), half:D_MODEL] = r[1:2] + r[3:4]
        return carry

    lax.fori_loop(0, tb // TOK_GROUP, group, 0)
    m = mod_ref[0, 0]
    x2_ref[...] = _ln(alpha * x1_ref[...] + m[5:6] * acc_ref[...]) * l2w_ref[...] + l2b_ref[...]


def _peer_out_call(idx, w2, x1, mod, ln2_w, ln2_b, tab, alpha, tb, t_len, ctx_len):
    n, d = x1.shape
    seg = lambda i: (((i * tb) % t_len) >= ctx_len).astype(jnp.int32)
    return pl.pallas_call(
        functools.partial(_peer_out_kernel, alpha=alpha),
        grid=(n // tb,),
        in_specs=[pl.BlockSpec((tb * NSEL,), lambda i: (i,), memory_space=pltpu.SMEM),
                  pl.BlockSpec((tb, 2 * NSEL), lambda i: (i, 0)),
                  pl.BlockSpec((tb, d), lambda i: (i, 0)),
                  pl.BlockSpec((1, 1, 6, d), lambda i: ((i * tb) // t_len, seg(i), 0, 0)),
                  pl.BlockSpec((1, d), lambda i: (0, 0)),
                  pl.BlockSpec((1, d), lambda i: (0, 0)),
                  pl.BlockSpec(memory_space=pltpu.VMEM)],
        out_specs=pl.BlockSpec((tb, d), lambda i: (i, 0)),
        out_shape=jax.ShapeDtypeStruct((n, d), F32),
        scratch_shapes=[pltpu.VMEM((TOK_GROUP, PACK_ROWS * NSEL, LANES), jnp.int32),
                        pltpu.VMEM((tb, d), F32)],
        compiler_params=_cparams(1),
        name="peer_out",
    )(idx, w2, x1, mod, ln2_w.reshape(1, d), ln2_b.reshape(1, d), tab)


def _rope_tables(n_rows, ctx_len):
    row = jnp.repeat(jnp.arange(n_rows), GRID_W).astype(F32)
    col = jnp.tile(jnp.arange(GRID_W), n_rows).astype(F32)
    quarter = HEAD_DIM // 4
    inv_freq = ROPE_BASE ** (-2.0 * jnp.arange(quarter, dtype=F32) / (HEAD_DIM // 2))
    ang_r = row[:, None] * inv_freq
    ang_c = col[:, None] * inv_freq
    cr, sr, cc, sc = jnp.cos(ang_r), jnp.sin(ang_r), jnp.cos(ang_c), jnp.sin(ang_c)
    cos = jnp.concatenate([cr, cr, cc, cc], axis=-1)
    sin = jnp.concatenate([-sr, sr, -sc, sc], axis=-1)
    cos = jnp.concatenate([jnp.ones((ctx_len, HEAD_DIM), F32), cos], axis=0)
    sin = jnp.concatenate([jnp.zeros((ctx_len, HEAD_DIM), F32), sin], axis=0)
    reps = QK_COLS // HEAD_DIM
    return jnp.tile(cos, (1, reps)), jnp.tile(sin, (1, reps))


def _swap_cols():
    q = HEAD_DIM // 4
    one = np.concatenate([np.arange(q, 2 * q), np.arange(0, q), np.arange(3 * q, 4 * q), np.arange(2 * q, 3 * q)])
    return np.concatenate([one + HEAD_DIM * i for i in range(QK_COLS // HEAD_DIM)])


def kernel(x, c, ctx, c_ctx, ada_w, ada_b, w_in, rwkv_conv, rwkv_w0, rwkv_w2, rwkv_a0, rwkv_a2, rwkv_g2, rwkv_k_k, rwkv_k_a, rwkv_r_k, rwkv_ln_w, rwkv_ln_b, diff_lambda, diff_subln_w, hgrn_lb_logits, hgrn_norm_w, w_out, ln1_w, ln1_b, peer_wq, peer_subkeys, peer_u, peer_v, ln2_w, ln2_b):
    bsz, seq, d = x.shape
    ctx_len = ctx.shape[1]
    depth = w_in.shape[0]
    t_len = ctx_len + seq
    tm = 256 if ctx_len % 256 == 0 else 128
    assert ctx_len % tm == 0 and seq % tm == 0 and seq % GRID_W == 0
    ctx_blocks = ctx_len // tm
    alpha = (2.0 * depth) ** 0.25

    consts = _scan_consts()
    hsum = jnp.asarray((np.arange(A_WIDTH)[:, None] // HEAD_DIM == np.arange(A_WIDTH)[None, :] // HEAD_DIM)
                       .astype(np.float32))
    cos, sin = _rope_tables(seq // GRID_W, ctx_len)
    swap = _swap_cols()

    lb_p = jax.nn.softmax(hgrn_lb_logits.astype(F32), axis=0)
    lower_bounds = jnp.cumsum(lb_p, axis=0) - lb_p[0]

    n_cond = 8 * ((bsz + 1 + 7) // 8)
    cc = jnp.zeros((n_cond, d), F32).at[0].set(c_ctx).at[1:1 + bsz].set(c)
    mods = _ada_call(cc, ada_w, ada_b)

    xs = jnp.concatenate([ctx, x], axis=1)
    for l in range(depth):
        last = l == depth - 1
        m = mods[l].reshape(n_cond, 6, d)
        mod = jnp.stack([jnp.broadcast_to(m[0], (bsz, 6, d)), m[1:1 + bsz]], axis=1)
        q_cols = w_in[l][:, A_COLS:A_COLS + QK_COLS]
        k_cols = w_in[l][:, A_COLS + QK_COLS:A_COLS + 2 * QK_COLS]
        w_ext = jnp.concatenate([w_in[l], q_cols[:, swap], k_cols[:, swap]], axis=1).astype(BF16)
        P = dict(rwkv_conv=rwkv_conv[l], rwkv_w0=rwkv_w0[l], rwkv_w2=rwkv_w2[l], rwkv_a0=rwkv_a0[l],
                 rwkv_a2=rwkv_a2[l], rwkv_g2=rwkv_g2[l], rwkv_k_k=rwkv_k_k[l], rwkv_k_a=rwkv_k_a[l],
                 rwkv_r_k=rwkv_r_k[l], rwkv_ln_w=rwkv_ln_w[l], rwkv_ln_b=rwkv_ln_b[l],
                 hgrn_norm_w=hgrn_norm_w[l], w_out_bf16=w_out[l].astype(BF16),
                 ln1_w=ln1_w[l], ln1_b=ln1_b[l], peer_wq_bf16=peer_wq[l].astype(BF16))

        oa, q, k, v, oc = _inproj_call(xs, mod, w_ext, cos, sin, tm, ctx_blocks)
        r, k2, vv, avec, bvec, lw, g, bonus = _rwkv_prep_call(oa, P, hsum, tm, ctx_blocks)
        rw_o = _rwkv_scan_call(r, k2, vv, avec, bvec, lw, consts, ctx_len // CHUNK)
        hg_o = _hgrn_scan_call(oc, lower_bounds[l], consts, ctx_len // CHUNK)
        lam_init = 0.8 - 0.6 * math.exp(-0.3 * l)
        row0 = ctx_blocks if last else 0
        t_out = t_len - row0 * tm
        yb = _attn_call(q, k, v, diff_lambda[l], diff_subln_w[l], lam_init, tm, ctx_len, skip_ctx=last)
        x1, u2, qp = _outproj_call(xs, mod, rw_o, g, bonus, yb, hg_o, oc, hsum, P, alpha, tm, ctx_blocks, row0)

        n_tok = bsz * t_out
        tb_route = 128
        e_t, gate_t = _peer_route_call(qp.reshape(n_tok, -1), peer_subkeys[l], tb_route)
        idx = (jnp.swapaxes(e_t, 1, 2).reshape(n_tok * NSEL) * PACK_ROWS).astype(jnp.int32)
        gate2 = jnp.repeat(jnp.swapaxes(gate_t, 1, 2).reshape(n_tok, NSEL), 2, axis=1)
        tb = 64
        w2 = _peer_act_call(idx, u2.reshape(n_tok, d), gate2, _pack_table(peer_u[l]), tb)
        x2 = _peer_out_call(idx, w2, x1.reshape(n_tok, d), mod, ln2_w[l], ln2_b[l], _pack_table(peer_v[l]),
                            alpha, tb, t_out, ctx_len - row0 * tm)
        xs = x2.reshape(bsz, t_out, d)
    return xs
```

```python
import functools
import math

import numpy as np
import jax
import jax.numpy as jnp
from jax import lax
from jax.experimental import pallas as pl
from jax.experimental.pallas import tpu as pltpu

F32 = jnp.float32
BF16 = jnp.bfloat16
HIGHEST = lax.Precision.HIGHEST

D_MODEL = 1024
GRID_W = 64
HEAD_DIM = 64
A_WIDTH = 256
A_DECAY_LORA = 32
A_ICLR_LORA = 32
A_GATE_LORA = 64
B_WIDTH = 512
B_HEADS = 4
B_V_DIM = 128
C_WIDTH = 256
ROPE_BASE = 10000.0
MIN_FORGET = 1e-30
PEER_HEADS = 8
PEER_NKEYS = 128
PEER_TOPK = 16
PEER_QDIM = 256
LN_EPS = 1e-5
RWKV_GN_EPS = 64e-5

A_COLS = 3 * A_WIDTH + A_DECAY_LORA + A_ICLR_LORA + A_GATE_LORA
QK_COLS = 2 * B_HEADS * HEAD_DIM
C_COLS = 5 * C_WIDTH
IN_WIDTH = A_COLS + 3 * QK_COLS + C_COLS
SCAN_HEADS = 4
CHUNK = 64
STACK = SCAN_HEADS * CHUNK
LANES = 128
VMEM_LIMIT = 56 * 1024 * 1024


def _cparams(n_axes, vmem=None):
    return pltpu.CompilerParams(dimension_semantics=("arbitrary",) * n_axes,
                                vmem_limit_bytes=vmem or VMEM_LIMIT)


def _ln(x):
    xc = x - jnp.mean(x, axis=-1, keepdims=True)
    return xc * lax.rsqrt(jnp.mean(xc * xc, axis=-1, keepdims=True) + LN_EPS)


def _mm(a, b, dims=((1,), (0,)), exact=False):
    dn = (dims, ((), ()))
    if exact is True:
        return lax.dot_general(a, b, dn, precision=HIGHEST, preferred_element_type=F32)
    dot = lambda p, q: lax.dot_general(p, q, dn, preferred_element_type=F32)
    a_hi = a.astype(BF16)
    b_hi = b.astype(BF16)
    if exact == "rhs3":
        r1 = b - b_hi.astype(F32)
        b_mid = r1.astype(BF16)
        b_lo = (r1 - b_mid.astype(F32)).astype(BF16)
        return dot(a_hi, b_hi) + (dot(a_hi, b_mid) + dot(a_hi, b_lo))
    if exact == "x3":
        a_lo = (a - a_hi.astype(F32)).astype(BF16)
        b_lo = (b - b_hi.astype(F32)).astype(BF16)
        return dot(a_hi, b_hi) + (dot(a_hi, b_lo) + dot(a_lo, b_hi))
    return dot(a_hi, b_hi)


NT = ((1,), (1,))
TN = ((0,), (0,))


def _ada_kernel(c_ref, w_ref, b_ref, o_ref):
    c = c_ref[...]
    s = c * jax.nn.sigmoid(c)
    o_ref[...] = _mm(s, w_ref[...], exact=True) + b_ref[...]


def _ada_call(cc, ada_w, ada_b):
    depth, d, n = ada_w.shape
    bn = 512
    return pl.pallas_call(
        _ada_kernel,
        grid=(depth, n // bn),
        in_specs=[pl.BlockSpec(cc.shape, lambda l, j: (0, 0)),
                  pl.BlockSpec((None, d, bn), lambda l, j: (l, 0, j)),
                  pl.BlockSpec((None, 1, bn), lambda l, j: (l, 0, j))],
        out_specs=pl.BlockSpec((None, cc.shape[0], bn), lambda l, j: (l, 0, j)),
        out_shape=jax.ShapeDtypeStruct((depth, cc.shape[0], n), F32),
        compiler_params=_cparams(2),
        name="ada_mod",
    )(cc, ada_w, ada_b.reshape(depth, 1, n))


def _inproj_kernel(x_ref, mod_ref, w_ref, cos_ref, sin_ref, oa_ref, q_ref, k_ref, v_ref, oc_ref):
    m = mod_ref[0, 0]
    u = _ln(x_ref[0]) * (1.0 + m[1:2]) + m[0:1]
    h = _mm(u, w_ref[...])
    oa_ref[0] = h[:, 0:A_COLS]
    cos = cos_ref[...]
    sin = sin_ref[...]
    q0 = A_COLS
    k0 = q0 + QK_COLS
    v0 = k0 + QK_COLS
    c0 = v0 + QK_COLS
    qs0 = IN_WIDTH
    ks0 = qs0 + QK_COLS
    q = (h[:, q0:k0] * cos + h[:, qs0:ks0] * sin) * (math.log2(math.e) * HEAD_DIM ** -0.5)
    k = h[:, k0:v0] * cos + h[:, ks0:ks0 + QK_COLS] * sin
    q_ref[0] = q.astype(BF16)
    k_ref[0] = k.astype(BF16)
    v_ref[0] = h[:, v0:c0].astype(BF16)
    oc_ref[0] = h[:, c0:IN_WIDTH]


def _inproj_call(x, mod, w_ext, cos, sin, tm, ctx_blocks):
    b, t, d = x.shape
    n_ext = w_ext.shape[1]
    tok = lambda width: pl.BlockSpec((1, tm, width), lambda i, j: (i, j, 0))
    return pl.pallas_call(
        _inproj_kernel,
        grid=(b, t // tm),
        in_specs=[tok(d),
                  pl.BlockSpec((1, 1, 6, d), lambda i, j: (i, (j >= ctx_blocks).astype(jnp.int32), 0, 0)),
                  pl.BlockSpec((d, n_ext), lambda i, j: (0, 0)),
                  pl.BlockSpec((tm, QK_COLS), lambda i, j: (j, 0)),
                  pl.BlockSpec((tm, QK_COLS), lambda i, j: (j, 0))],
        out_specs=[tok(A_COLS), tok(QK_COLS), tok(QK_COLS), tok(QK_COLS), tok(C_COLS)],
        out_shape=[jax.ShapeDtypeStruct((b, t, A_COLS), F32),
                   jax.ShapeDtypeStruct((b, t, QK_COLS), BF16),
                   jax.ShapeDtypeStruct((b, t, QK_COLS), BF16),
                   jax.ShapeDtypeStruct((b, t, QK_COLS), BF16),
                   jax.ShapeDtypeStruct((b, t, C_COLS), F32)],
        compiler_params=_cparams(2),
        name="in_proj",
    )(x, mod, w_ext, cos, sin)


def _rwkv_prep_kernel(cur_ref, prev_ref, next_ref, conv_ref, w0_ref, w2_ref, a0_ref, a2_ref, g2_ref,
                      kk_ref, ka_ref, rk_ref, hsum_ref,
                      r_o, k_o, v_o, a_o, b_o, lw_o, g_o, bonus_o, *, ctx_blocks, n_blocks):
    j = pl.program_id(1)
    cur = cur_ref[0]
    tm = cur.shape[0]
    w3 = 3 * A_WIDTH
    rkv = cur[:, 0:w3]
    has_prev = jnp.logical_and(j != 0, j != ctx_blocks)
    has_next = jnp.logical_and(j != ctx_blocks - 1, j != n_blocks - 1)
    prev_row = jnp.where(has_prev, prev_ref[0][7:8, 0:w3], 0.0)
    next_row = jnp.where(has_next, next_ref[0][0:1, 0:w3], 0.0)
    row = lax.broadcasted_iota(jnp.int32, (tm, w3), 0)
    xm1 = jnp.where(row == 0, prev_row, pltpu.roll(rkv, 1, 0))
    xp1 = jnp.where(row == tm - 1, next_row, pltpu.roll(rkv, tm - 1, 0))
    cw = conv_ref[...]
    conv = cw[0:1] * xm1 + cw[1:2] * rkv + cw[2:3] * xp1
    r = conv[:, 0:A_WIDTH]
    k = conv[:, A_WIDTH:2 * A_WIDTH]
    v = conv[:, 2 * A_WIDTH:w3]
    o = w3
    w_lo = cur[:, o:o + A_DECAY_LORA]
    a_lo = cur[:, o + A_DECAY_LORA:o + A_DECAY_LORA + A_ICLR_LORA]
    g_lo = cur[:, o + A_DECAY_LORA + A_ICLR_LORA:A_COLS]
    a = jax.nn.sigmoid(a0_ref[...] + _mm(a_lo, a2_ref[...], exact=True))
    g = _mm(jax.nn.sigmoid(g_lo), g2_ref[...], exact=True)
    hsum = hsum_ref[...]
    kk = k * kk_ref[...]
    ss = _mm(kk * kk, hsum, exact=True)
    kk = kk * lax.rsqrt(jnp.maximum(ss, 1e-24))
    k2 = k * (1.0 + (a - 1.0) * ka_ref[...])
    tw = jnp.tanh(w_lo)
    w0 = w0_ref[...]
    for d in range(2):
        wl = w0[d:d + 1] + _mm(tw, w2_ref[d], exact=True)
        lw_o[d, 0] = -math.exp(-0.5) * jax.nn.sigmoid(wl)
    r_o[0] = r
    k_o[0] = k2
    v_o[0] = v
    a_o[0] = -kk
    b_o[0] = kk * a
    g_o[0] = g
    bonus_o[0] = _mm(r * k2 * rk_ref[...], hsum, exact=True) * v


def _rwkv_prep_call(oa, P, hsum, tm, ctx_blocks):
    b, t, _ = oa.shape
    n_blocks = t // tm
    per8 = tm // 8
    last8 = t // 8 - 1
    full = lambda arr: pl.BlockSpec(arr.shape, lambda i, j: (0,) * arr.ndim)
    tok = pl.BlockSpec((1, tm, A_WIDTH), lambda i, j: (i, j, 0))
    row = lambda p: p.reshape(1, -1)
    params = [P['rwkv_conv'], P['rwkv_w0'], P['rwkv_w2'], row(P['rwkv_a0']), P['rwkv_a2'], P['rwkv_g2'],
              row(P['rwkv_k_k']), row(P['rwkv_k_a']), row(P['rwkv_r_k']), hsum]
    shp = jax.ShapeDtypeStruct((b, t, A_WIDTH), F32)
    return pl.pallas_call(
        functools.partial(_rwkv_prep_kernel, ctx_blocks=ctx_blocks, n_blocks=n_blocks),
        grid=(b, n_blocks),
        in_specs=[pl.BlockSpec((1, tm, A_COLS), lambda i, j: (i, j, 0)),
                  pl.BlockSpec((1, 8, A_COLS), lambda i, j: (i, jnp.maximum(j * per8 - 1, 0), 0)),
                  pl.BlockSpec((1, 8, A_COLS), lambda i, j: (i, jnp.minimum((j + 1) * per8, last8), 0))]
                 + [full(p) for p in params],
        out_specs=[tok, tok, tok, tok, tok,
                   pl.BlockSpec((2, 1, tm, A_WIDTH), lambda i, j: (0, i, j, 0)), tok, tok],
        out_shape=[shp, shp, shp, shp, shp, jax.ShapeDtypeStruct((2, b, t, A_WIDTH), F32), shp, shp],
        compiler_params=_cparams(2),
        name="rwkv_prep",
    )(oa, oa, oa, *params)


def _scan_consts():
    c = CHUNK
    t = np.arange(c)
    tri = np.zeros((2, c, c), np.float32)
    tri[0] = (t[None, :] <= t[:, None])
    tri[1] = (t[None, :] >= t[:, None])
    st = np.arange(STACK)
    same_head = (st[:, None] // c) == (st[None, :] // c)
    tt = st[:, None] % c
    ss = st[None, :] % c
    strict = np.stack([same_head & (ss < tt), same_head & (ss > tt)]).astype(np.float32)
    incl = np.stack([same_head & (ss <= tt), same_head & (ss >= tt)]).astype(np.float32)
    head_mask = ((st[:, None] // c) == (np.arange(SCAN_HEADS * HEAD_DIM)[None, :] // HEAD_DIM)).astype(np.float32)
    n_lv = int(math.log2(c))
    seg = np.zeros((2, (n_lv + 1) * c, c), np.float32)
    seg[0, :c] = tri[0]
    seg[1, :c] = tri[1]
    level = -np.ones((2, STACK, STACK), np.int32)
    for d in range(2):
        level[d][same_head & (tt == ss)] = 0
    for l in range(1, n_lv + 1):
        n = 2 ** l
        for ti in range(c):
            s0 = (ti // n) * n
            m = s0 + n // 2 - 1
            hh = s0 + n // 2
            if ti > m:
                seg[0, l * c + ti, m + 1:ti + 1] = 1.0
            else:
                seg[0, l * c + ti, ti + 1:m + 1] = 1.0
            if ti < hh:
                seg[1, l * c + ti, ti:hh] = 1.0
            else:
                seg[1, l * c + ti, hh:ti] = 1.0
        same_blk = (tt // n) == (ss // n)
        t_second = (tt % n) >= n // 2
        s_second = (ss % n) >= n // 2
        level[0][same_head & same_blk & t_second & ~s_second] = l
        level[1][same_head & same_blk & ~t_second & s_second] = l
    return dict(tri=jnp.asarray(tri), strict=jnp.asarray(strict), incl=jnp.asarray(incl),
                head_mask=jnp.asarray(head_mask), seg=jnp.asarray(seg), level=jnp.asarray(level))


def _tile4(x):
    return jnp.concatenate([x] * SCAN_HEADS, axis=0)


def _fold4(x):
    c = CHUNK
    return x[0:c] + x[c:2 * c] + x[2 * c:3 * c] + x[3 * c:4 * c]


def _chunk_index(d, j, nc_ctx, nc):
    if d == 0:
        return j
    return jnp.where(j < nc_ctx, nc_ctx - 1 - j, nc - 1 - (j - nc_ctx))


def _rwkv_chunk(tri, strict, incl, hm, r, k, v, a, b, lw, z, mm, mmi):
    cum = _mm(tri, lw, exact="rhs3")
    tot = jnp.sum(lw, axis=0, keepdims=True)
    e_neg = jnp.exp(-cum)
    e_end = jnp.exp(tot - cum)
    a_sm = _tile4(a * jnp.exp(cum - lw)) * hm
    r_sm = _tile4(r * jnp.exp(cum)) * hm
    v_sm = _tile4(v) * hm
    b_rep = _tile4(b * e_neg)
    k_rep = _tile4(k * e_neg)
    zero = jnp.zeros((STACK, STACK), F32)
    n_ab = jnp.where(strict, mm(a_sm, b_rep, NT), zero)
    n_ak = jnp.where(strict, mm(a_sm, k_rep, NT), zero)
    n_rb = jnp.where(incl, mm(r_sm, b_rep, NT), zero)
    n_rk = jnp.where(incl, mm(r_sm, k_rep, NT), zero)
    rows = lax.broadcasted_iota(jnp.int32, (STACK, STACK), 0)
    cols = lax.broadcasted_iota(jnp.int32, (STACK, STACK), 1)
    inv = jnp.where(rows == cols, 1.0, 0.0) + n_ab
    npow = n_ab
    for _ in range(int(math.log2(CHUNK)) - 1):
        npow = mmi(npow, npow)
        inv = inv + mmi(inv, npow)
    u_sm = mm(inv, mm(a_sm, z, NT) + mm(n_ak, v_sm))
    o_sm = mm(r_sm, z, NT) + mm(n_rb, u_sm) + mm(n_rk, v_sm)
    z_new = z * jnp.exp(tot) + mm(u_sm, _tile4(b * e_end) * hm, TN) + mm(v_sm, _tile4(k * e_end) * hm, TN)
    return _fold4(o_sm), z_new


def _rwkv_scan_kernel(tri_ref, strict_ref, incl_ref, hm_ref, *refs, exact, inv_exact):
    ins, (of_ref, ob_ref, z_ref) = refs[:12], refs[12:]

    @pl.when(pl.program_id(1) == 0)
    def _():
        z_ref[...] = jnp.zeros_like(z_ref)

    mm = functools.partial(_mm, exact=exact)
    mmi = functools.partial(_mm, exact=inv_exact)
    hm = hm_ref[...]
    for d, o_ref in enumerate((of_ref, ob_ref)):
        r, k, v, a, b = (ref[0] for ref in ins[6 * d:6 * d + 5])
        o, z_new = _rwkv_chunk(tri_ref[d], strict_ref[d] > 0.0, incl_ref[d] > 0.0, hm, r, k, v, a, b,
                               ins[6 * d + 5][0, 0], z_ref[d], mm, mmi)
        o_ref[0] = o
        z_ref[d] = z_new


def _rwkv_scan_call(r, k, v, a, bvec, lw, consts, nc_ctx, exact=False, inv_exact=False):
    b, t, w = r.shape
    nc = t // CHUNK
    full = lambda arr: pl.BlockSpec(arr.shape, lambda i, j: (0,) * arr.ndim)
    specs, args = [], []
    for d in range(2):
        cidx = functools.partial(_chunk_index, d, nc_ctx=nc_ctx, nc=nc)
        tok = pl.BlockSpec((1, CHUNK, w), lambda i, j, cidx=cidx: (i, cidx(j), 0))
        specs += [tok] * 5 + [pl.BlockSpec((1, 1, CHUNK, w), lambda i, j, cidx=cidx, d=d: (d, i, cidx(j), 0))]
        args += [r, k, v, a, bvec, lw]
    out_specs = [pl.BlockSpec((1, CHUNK, w), lambda i, j, d=d: (i, _chunk_index(d, j, nc_ctx, nc), 0))
                 for d in range(2)]
    cs = [consts['tri'], consts['strict'], consts['incl'], consts['head_mask']]
    return pl.pallas_call(
        functools.partial(_rwkv_scan_kernel, exact=exact, inv_exact=inv_exact),
        grid=(b, nc),
        in_specs=[full(c) for c in cs] + specs,
        out_specs=out_specs,
        out_shape=[jax.ShapeDtypeStruct((b, t, w), F32)] * 2,
        scratch_shapes=[pltpu.VMEM((2, STACK, STACK), F32)],
        compiler_params=_cparams(2),
        name="rwkv_scan",
    )(*cs, *args)


def _hgrn_chunk(seg, level, hm, lb, qr, zf, v, z, mm):
    q = qr * jax.nn.sigmoid(qr)
    f = lb + (1.0 - lb) * jax.nn.sigmoid(zf)
    logf = jnp.log(jnp.maximum(f, MIN_FORGET))
    k = (1.0 - lb) * jax.nn.sigmoid(-zf)
    segs = _mm(seg, logf, exact="rhs3")
    cum = segs[0:CHUNK]
    tot = jnp.sum(logf, axis=0, keepdims=True)
    attn = jnp.where(level == 0, mm(_tile4(q) * hm, _tile4(k), NT), 0.0)
    for l in range(1, int(math.log2(CHUNK)) + 1):
        e = jnp.exp(segs[l * CHUNK:(l + 1) * CHUNK])
        attn = jnp.where(level == l, mm(_tile4(q * e) * hm, _tile4(k * e), NT), attn)
    v_sm = _tile4(v) * hm
    o_sm = mm(_tile4(q * jnp.exp(cum)) * hm, z, NT) + mm(attn, v_sm)
    z_new = z * jnp.exp(tot) + mm(v_sm, _tile4(k * jnp.exp(tot - cum)) * hm, TN)
    return _fold4(o_sm), z_new


def _hgrn_scan_kernel(seg_ref, level_ref, hm_ref, lb_ref, *refs, exact):
    ins, (of_ref, ob_ref, z_ref) = refs[:6], refs[6:]

    @pl.when(pl.program_id(1) == 0)
    def _():
        z_ref[...] = jnp.zeros_like(z_ref)

    mm = functools.partial(_mm, exact=exact)
    hm = hm_ref[...]
    lb = lb_ref[...]
    for d, o_ref in enumerate((of_ref, ob_ref)):
        qr, zf, v = (ref[0] for ref in ins[3 * d:3 * d + 3])
        o, z_new = _hgrn_chunk(seg_ref[d], level_ref[d], hm, lb, qr, zf, v, z_ref[d], mm)
        o_ref[0] = o
        z_ref[d] = z_new


def _hgrn_scan_call(oc, lb, consts, nc_ctx, exact=False):
    b, t, _ = oc.shape
    w = C_WIDTH
    nc = t // CHUNK
    full = lambda arr: pl.BlockSpec(arr.shape, lambda i, j: (0,) * arr.ndim)
    specs = []
    for d in range(2):
        cidx = functools.partial(_chunk_index, d, nc_ctx=nc_ctx, nc=nc)
        specs += [pl.BlockSpec((1, CHUNK, w), lambda i, j, cidx=cidx, col=col: (i, cidx(j), col))
                  for col in (0, 1 + d, 3)]
    out_specs = [pl.BlockSpec((1, CHUNK, w), lambda i, j, d=d: (i, _chunk_index(d, j, nc_ctx, nc), 0))
                 for d in range(2)]
    cs = [consts['seg'], consts['level'], consts['head_mask'], lb.reshape(1, w)]
    return pl.pallas_call(
        functools.partial(_hgrn_scan_kernel, exact=exact),
        grid=(b, nc),
        in_specs=[full(c) for c in cs] + specs,
        out_specs=out_specs,
        out_shape=[jax.ShapeDtypeStruct((b, t, w), F32)] * 2,
        scratch_shapes=[pltpu.VMEM((2, STACK, STACK), F32)],
        compiler_params=_cparams(2),
        name="hgrn_scan",
    )(*cs, *([oc] * 6))


def _attn_kernel(lam_ref, w_ref, q_ref, k_ref, v_ref, o_ref, *, lam_init, ctx_len, ctx_blocks):
    j = pl.program_id(2)
    ll = lam_ref[...]
    lam = (jnp.exp(jnp.sum(ll[0:1] * ll[1:2], axis=-1, keepdims=True))
           - jnp.exp(jnp.sum(ll[2:3] * ll[3:4], axis=-1, keepdims=True)) + lam_init)
    q = q_ref[0]
    first = lax.broadcasted_iota(jnp.int32, q.shape, 1) < HEAD_DIM
    zq = jnp.zeros_like(q)
    q1 = jnp.where(first, q, zq)
    q2 = jnp.where(first, zq, q)

    def attend(k, v):
        def softmax_parts(qm):
            s = lax.dot_general(qm, k, (NT, ((), ())), preferred_element_type=F32)
            p = jnp.exp2(s - jnp.max(s, axis=-1, keepdims=True))
            return p, 1.0 / jnp.sum(p, axis=-1, keepdims=True)
        p1, i1 = softmax_parts(q1)
        p2, i2 = softmax_parts(q2)
        a = p1 * i1 - p2 * (lam * i2)
        o = jnp.dot(a.astype(BF16), v, preferred_element_type=F32)
        y = o * lax.rsqrt(jnp.mean(o * o, axis=-1, keepdims=True) + LN_EPS)
        o_ref[0] = y * w_ref[...] * (1.0 - lam_init)

    @pl.when(j < ctx_blocks)
    def _():
        attend(k_ref[0, 0:ctx_len], v_ref[0, 0:ctx_len])

    @pl.when(j >= ctx_blocks)
    def _():
        attend(k_ref[0], v_ref[0])


def _attn_call(q, k, v, diff_lambda, subln_w, lam_init, tq, ctx_len, skip_ctx):
    b, t, _ = q.shape
    ctx_blocks = ctx_len // tq
    j0 = ctx_blocks if skip_ctx else 0
    kv = pl.BlockSpec((1, t, B_V_DIM), lambda i, h, j: (i, 0, h))
    qo = pl.BlockSpec((1, tq, B_V_DIM), lambda i, h, j: (i, j + j0, h))
    return pl.pallas_call(
        functools.partial(_attn_kernel, lam_init=lam_init, ctx_len=ctx_len, ctx_blocks=ctx_blocks - j0),
        grid=(b, B_HEADS, t // tq - j0),
        in_specs=[pl.BlockSpec((4, HEAD_DIM), lambda i, h, j: (0, 0)),
                  pl.BlockSpec((1, B_V_DIM), lambda i, h, j: (0, 0)),
                  qo, kv, kv],
        out_specs=qo,
        out_shape=jax.ShapeDtypeStruct((b, t, B_WIDTH), F32),
        compiler_params=_cparams(3),
        name="diff_attn",
    )(diff_lambda, subln_w.reshape(1, B_V_DIM), q, k, v)


def _outproj_kernel(x_ref, mod_ref, of_ref, ob_ref, g_ref, bonus_ref, yb_ref, cf_ref, cb_ref, cg_ref,
                    hsum_ref, lnw_ref, lnb_ref, hw_ref, wout_ref, l1w_ref, l1b_ref, wq_ref,
                    x1_ref, u2_ref, qp_ref, *, alpha):
    m = mod_ref[0, 0]
    hsum = hsum_ref[...]
    inv = 1.0 / HEAD_DIM
    o = of_ref[0] + ob_ref[0]
    oc = o - _mm(o, hsum, exact=True) * inv
    ya = oc * lax.rsqrt(_mm(oc * oc, hsum, exact=True) * inv + RWKV_GN_EPS)
    ya = (ya * lnw_ref[...] + lnb_ref[...] + bonus_ref[0]) * g_ref[0]
    c = cf_ref[0] + cb_ref[0]
    cg = cg_ref[0]
    yc = c * lax.rsqrt(_mm(c * c, hsum, exact=True) * inv + LN_EPS) * hw_ref[...] * (cg * jax.nn.sigmoid(cg))
    y = jnp.concatenate([ya, yb_ref[0], yc], axis=-1)
    proj = _mm(y, wout_ref[...])
    x1 = _ln(alpha * x_ref[0] + m[2:3] * proj) * l1w_ref[...] + l1b_ref[...]
    x1_ref[0] = x1
    u2 = _ln(x1) * (1.0 + m[4:5]) + m[3:4]
    u2_ref[0] = u2
    qp_ref[0] = _mm(u2, wq_ref[...])


def _outproj_call(x, mod, rw_o, rw_g, rw_bonus, yb, hg_o, oc, hsum, P, alpha, tm, ctx_blocks, row0):
    b, t, d = x.shape
    nq = P['peer_wq_bf16'].shape[1]
    t_out = t - row0 * tm
    tok = lambda width, col=0: pl.BlockSpec((1, tm, width), lambda i, j: (i, j + row0, col))
    otok = lambda width: pl.BlockSpec((1, tm, width), lambda i, j: (i, j, 0))
    full = lambda arr: pl.BlockSpec(arr.shape, lambda i, j: (0,) * arr.ndim)
    row = lambda p: p.reshape(1, -1)
    params = [hsum, row(P['rwkv_ln_w']), row(P['rwkv_ln_b']), row(jnp.tile(P['hgrn_norm_w'], SCAN_HEADS)),
              P['w_out_bf16'], row(P['ln1_w']), row(P['ln1_b']), P['peer_wq_bf16']]
    return pl.pallas_call(
        functools.partial(_outproj_kernel, alpha=alpha),
        grid=(b, t_out // tm),
        in_specs=[tok(d),
                  pl.BlockSpec((1, 1, 6, d), lambda i, j: (i, (j + row0 >= ctx_blocks).astype(jnp.int32), 0, 0)),
                  tok(A_WIDTH), tok(A_WIDTH), tok(A_WIDTH), tok(A_WIDTH), tok(B_WIDTH),
                  tok(C_WIDTH), tok(C_WIDTH), tok(C_WIDTH, 4)] + [full(p) for p in params],
        out_specs=[otok(d), otok(d), otok(nq)],
        out_shape=[jax.ShapeDtypeStruct((b, t_out, d), F32), jax.ShapeDtypeStruct((b, t_out, d), F32),
                   jax.ShapeDtypeStruct((b, t_out, nq), F32)],
        compiler_params=_cparams(2),
        name="out_proj",
    )(x, mod, rw_o[0], rw_o[1], rw_g, rw_bonus, yb, hg_o[0], hg_o[1], oc, *params)


def _topk_rows(s, order=None, payload=None):
    if order is None:
        order = lax.broadcasted_iota(jnp.int32, s.shape, 0)
    big = jnp.int32(2 ** 30)
    vals, picks = [], []
    for _ in range(PEER_TOPK):
        m = jnp.max(s, axis=0, keepdims=True)
        idx = jnp.min(jnp.where(s == m, order, big), axis=0, keepdims=True)
        hit = order == idx
        vals.append(m)
        picks.append(idx if payload is None else jnp.max(jnp.where(hit, payload, -1), axis=0, keepdims=True))
        s = jnp.where(hit, -jnp.inf, s)
    return jnp.concatenate(vals, axis=0), jnp.concatenate(picks, axis=0)


def _pair_candidates(sv, si):
    k = PEER_TOPK
    vals, flat, eid = [], [], []
    tokens = sv[0].shape[1]
    row8 = lax.broadcasted_iota(jnp.int32, (8, tokens), 0)
    row16 = lax.broadcasted_iota(jnp.int32, (k, tokens), 0)
    neg = -jnp.inf
    for i in range(8):
        n_valid = k // (i + 1)
        rows, row = (k, row16) if n_valid > 8 else (8, row8)
        v = sv[0][i:i + 1] + sv[1][0:rows]
        vals.append(v if n_valid == rows else jnp.where(row < n_valid, v, neg))
        flat.append(i * k + row)
        eid.append(si[0][i:i + 1] * PEER_NKEYS + si[1][0:rows])
    vals.append(sv[0][8:k] + sv[1][0:1])
    flat.append((row8 + 8) * k)
    eid.append(si[0][8:k] * PEER_NKEYS + si[1][0:1])
    cat = lambda xs: jnp.concatenate(xs, axis=0)
    return cat(vals), cat(flat), cat(eid)


def _route_head(q_ref, keys_ref, e_ref, g_ref, h):
    half = PEER_QDIM // 2
    sv, si = [], []
    for p in range(2):
        start = pl.multiple_of((2 * h + p) * half, half)
        qhp = q_ref[:, pl.ds(start, half)]
        s = _mm(keys_ref[p], qhp, NT, exact=True)
        vals, idx = _topk_rows(s)
        sv.append(vals)
        si.append(idx)
    cand, flat, eid = _pair_candidates(sv, si)
    top, e = _topk_rows(cand, flat, eid)
    ex = jnp.exp(top - top[0:1])
    gate = ex / jnp.sum(ex, axis=0, keepdims=True)
    off = pl.multiple_of(h * PEER_TOPK, PEER_TOPK)
    e_ref[0, pl.ds(off, PEER_TOPK), :] = e
    g_ref[0, pl.ds(off, PEER_TOPK), :] = gate


def _peer_route_kernel(q_ref, keys_ref, e_ref, g_ref):
    def head(h, carry):
        _route_head(q_ref, keys_ref, e_ref, g_ref, h)
        return carry

    lax.fori_loop(0, PEER_HEADS, head, 0)


def _peer_route_call(qp, subkeys, tb, n_part):
    nq = qp.shape[1]
    nblk = n_part // tb
    nsel = PEER_HEADS * PEER_TOPK
    out = pl.BlockSpec((1, nsel, tb), lambda i: (i, 0, 0))
    return pl.pallas_call(
        _peer_route_kernel,
        grid=(nblk,),
        in_specs=[pl.BlockSpec((tb, nq), lambda i: (i, 0)),
                  pl.BlockSpec(subkeys.shape, lambda i: (0, 0, 0))],
        out_specs=[out, out],
        out_shape=[jax.ShapeDtypeStruct((nblk, nsel, tb), jnp.int32),
                   jax.ShapeDtypeStruct((nblk, nsel, tb), F32)],
        compiler_params=_cparams(1),
        name="peer_route",
    )(qp, subkeys)


PACK_ROWS = 4
NSEL = PEER_HEADS * PEER_TOPK


def _pack_table(tab):
    n, d = tab.shape
    bits = lax.bitcast_convert_type(tab.astype(BF16), jnp.uint16).astype(jnp.uint32)
    word = bits[:, :d // 2] | (bits[:, d // 2:] << 16)
    return lax.bitcast_convert_type(word, jnp.int32).reshape(n * PACK_ROWS, LANES)


PEER_PARTS = 8
TOK_GROUP = 8


def _gather_group(idx_ref, tab_ref, tile_ref, t0):
    tok_idx = [idx_ref.at[pl.ds(pl.multiple_of((t0 + u) * NSEL, NSEL), NSEL)] for u in range(TOK_GROUP)]
    for mi in range(NSEL):
        for u in range(TOK_GROUP):
            i = pl.multiple_of(tok_idx[u][mi], PACK_ROWS)
            tile_ref[u, pl.ds(PACK_ROWS * mi, PACK_ROWS), :] = tab_ref[pl.ds(i, PACK_ROWS), :]


def _packed_rows(tile_ref, u):
    g = jnp.concatenate([tile_ref[u, pl.ds(c, NSEL, stride=PACK_ROWS), :] for c in range(PACK_ROWS)], axis=1)
    return pltpu.bitcast(g, BF16)


def _split_hi_lo(x):
    hi = x.astype(BF16).astype(F32)
    return hi, x - hi


ROUTE_TB = 128


def _peer_act_kernel(idx_ref, u_ref, gate_ref, tab_ref, *refs, fuse_route):
    if fuse_route:
        q_ref, keys_ref, w_ref, e_ref, g_ref, tile_ref, act_ref = refs
    else:
        w_ref, tile_ref, act_ref = refs
    half = D_MODEL // 2
    sub = lax.broadcasted_iota(jnp.int32, (8, half), 0)
    even = (lax.broadcasted_iota(jnp.int32, (1, 2 * NSEL), 1) % 2) == 0
    groups_per_head = ROUTE_TB // (PEER_HEADS * TOK_GROUP)

    def step(h, carry):
        if fuse_route:
            _route_head(q_ref, keys_ref, e_ref, g_ref, h)
        for gi in range(groups_per_head):
            t0 = (h * groups_per_head + gi) * TOK_GROUP
            _gather_group(idx_ref, tab_ref, tile_ref.at[gi], t0)
            for u in range(TOK_GROUP):
                hi, lo = _split_hi_lo(u_ref[pl.ds(t0 + u, 1), :])
                lhs = jnp.where(sub == 0, hi[:, :half], jnp.where(sub == 1, hi[:, half:],
                      jnp.where(sub == 2, lo[:, :half], jnp.where(sub == 3, lo[:, half:], 0.0))))
                r = lax.dot_general(lhs.astype(BF16), _packed_rows(tile_ref.at[gi], u), (NT, ((), ())),
                                    preferred_element_type=F32)
                act_ref[pl.ds(t0 + u, 1), :] = jnp.where(even, r[0:1] + r[2:3], r[1:2] + r[3:4])
        return carry

    lax.fori_loop(0, PEER_HEADS, step, 0)
    part = act_ref[...]
    lane_even = (lax.broadcasted_iota(jnp.int32, part.shape, 1) % 2) == 0
    act = part + jnp.where(lane_even, pltpu.roll(part, 2 * NSEL - 1, 1), pltpu.roll(part, 1, 1))
    w_ref[...] = gate_ref[...] * (0.5 * act * (1.0 + lax.erf(act * math.sqrt(0.5))))


def _peer_act_call(idx, u2, gate2, tab, row0, n_part, route=None):
    d = u2.shape[1]
    tb = ROUTE_TB
    b0 = row0 // tb
    groups_per_head = tb // (PEER_HEADS * TOK_GROUP)
    in_specs = [pl.BlockSpec((tb * NSEL,), lambda i: (i,), memory_space=pltpu.SMEM),
                pl.BlockSpec((tb, d), lambda i: (i + b0, 0)),
                pl.BlockSpec((tb, 2 * NSEL), lambda i: (i, 0)),
                pl.BlockSpec(memory_space=pltpu.VMEM)]
    out_specs = [pl.BlockSpec((tb, 2 * NSEL), lambda i: (i, 0))]
    out_shape = [jax.ShapeDtypeStruct((n_part, 2 * NSEL), F32)]
    args = [idx, u2, gate2, tab]
    if route is not None:
        qp, subkeys, q_row0 = route
        qb0 = q_row0 // tb
        in_specs += [pl.BlockSpec((tb, qp.shape[1]), lambda i: (i + qb0, 0)),
                     pl.BlockSpec(subkeys.shape, lambda i: (0, 0, 0))]
        rout = pl.BlockSpec((1, NSEL, tb), lambda i: (i, 0, 0))
        out_specs += [rout, rout]
        out_shape += [jax.ShapeDtypeStruct((n_part // tb, NSEL, tb), jnp.int32),
                      jax.ShapeDtypeStruct((n_part // tb, NSEL, tb), F32)]
        args += [qp, subkeys]
    return pl.pallas_call(
        functools.partial(_peer_act_kernel, fuse_route=route is not None),
        grid=(n_part // tb,),
        in_specs=in_specs,
        out_specs=out_specs,
        out_shape=out_shape,
        scratch_shapes=[pltpu.VMEM((groups_per_head, TOK_GROUP, PACK_ROWS * NSEL, LANES), jnp.int32),
                        pltpu.VMEM((tb, 2 * NSEL), F32)],
        compiler_params=_cparams(1),
        name="peer_act_route" if route is not None else "peer_act",
    )(*args)


def _peer_out_kernel(idx_ref, w_ref, x1_ref, mod_ref, l2w_ref, l2b_ref, tab_ref, x2_ref, tile_ref, acc_ref,
                     *, alpha):
    tb = w_ref.shape[0]
    half = D_MODEL // 2
    sub = lax.broadcasted_iota(jnp.int32, (8, 2 * NSEL), 0)
    even = (lax.broadcasted_iota(jnp.int32, (8, 2 * NSEL), 1) % 2) == 0

    def group(gi, carry):
        t0 = gi * TOK_GROUP
        _gather_group(idx_ref, tab_ref, tile_ref, t0)
        for u in range(TOK_GROUP):
            hi, lo = _split_hi_lo(w_ref[pl.ds(t0 + u, 1), :])
            lhs = jnp.where((sub == 0) & even, hi, jnp.where((sub == 1) & ~even, hi,
                  jnp.where((sub == 2) & even, lo, jnp.where((sub == 3) & ~even, lo, 0.0))))
            r = jnp.dot(lhs.astype(BF16), _packed_rows(tile_ref, u), preferred_element_type=F32)
            acc_ref[pl.ds(t0 + u, 1), 0:half] = r[0:1] + r[2:3]
            acc_ref[pl.ds(t0 + u, 1), half:D_MODEL] = r[1:2] + r[3:4]
        return carry

    lax.fori_loop(0, tb // TOK_GROUP, group, 0)
    m = mod_ref[0, 0]
    x2_ref[...] = _ln(alpha * x1_ref[...] + m[5:6] * acc_ref[...]) * l2w_ref[...] + l2b_ref[...]


def _peer_out_call(idx, w2, x1, mod, ln2_w, ln2_b, tab, alpha, tb, t_len, ctx_len):
    n, d = x1.shape
    seg = lambda i: (((i * tb) % t_len) >= ctx_len).astype(jnp.int32)
    return pl.pallas_call(
        functools.partial(_peer_out_kernel, alpha=alpha),
        grid=(n // tb,),
        in_specs=[pl.BlockSpec((tb * NSEL,), lambda i: (i,), memory_space=pltpu.SMEM),
                  pl.BlockSpec((tb, 2 * NSEL), lambda i: (i, 0)),
                  pl.BlockSpec((tb, d), lambda i: (i, 0)),
                  pl.BlockSpec((1, 1, 6, d), lambda i: ((i * tb) // t_len, seg(i), 0, 0)),
                  pl.BlockSpec((1, d), lambda i: (0, 0)),
                  pl.BlockSpec((1, d), lambda i: (0, 0)),
                  pl.BlockSpec(memory_space=pltpu.VMEM)],
        out_specs=pl.BlockSpec((tb, d), lambda i: (i, 0)),
        out_shape=jax.ShapeDtypeStruct((n, d), F32),
        scratch_shapes=[pltpu.VMEM((TOK_GROUP, PACK_ROWS * NSEL, LANES), jnp.int32),
                        pltpu.VMEM((tb, d), F32)],
        compiler_params=_cparams(1),
        name="peer_out",
    )(idx, w2, x1, mod, ln2_w.reshape(1, d), ln2_b.reshape(1, d), tab)


def _num_parts(n_tok):
    return next(p for p in range(PEER_PARTS, 0, -1) if n_tok % (p * ROUTE_TB) == 0)


def _peer_ffn(qp, u2, x1, mod, subkeys, peer_u, peer_v, ln2_w, ln2_b, alpha, t_len, ctx_len):
    n_tok, d = u2.shape
    parts = _num_parts(n_tok)
    n_part = n_tok // parts
    tab_u = _pack_table(peer_u)
    e_t, gate_t = _peer_route_call(qp, subkeys, ROUTE_TB, n_part)
    idx_parts, w2_parts = [], []
    for p in range(parts):
        idx_p = (jnp.swapaxes(e_t, 1, 2).reshape(n_part * NSEL) * PACK_ROWS).astype(jnp.int32)
        gate2_p = jnp.repeat(jnp.swapaxes(gate_t, 1, 2).reshape(n_part, NSEL), 2, axis=1)
        idx_parts.append(idx_p)
        if p + 1 < parts:
            w2_p, e_t, gate_t = _peer_act_call(idx_p, u2, gate2_p, tab_u, p * n_part, n_part,
                                               route=(qp, subkeys, (p + 1) * n_part))
        else:
            w2_p, = _peer_act_call(idx_p, u2, gate2_p, tab_u, p * n_part, n_part)
        w2_parts.append(w2_p)
    idx = jnp.concatenate(idx_parts)
    w2 = jnp.concatenate(w2_parts, axis=0)
    return _peer_out_call(idx, w2, x1, mod, ln2_w, ln2_b, _pack_table(peer_v), alpha, 64, t_len, ctx_len)


def _rope_tables(n_rows, ctx_len):
    row = jnp.repeat(jnp.arange(n_rows), GRID_W).astype(F32)
    col = jnp.tile(jnp.arange(GRID_W), n_rows).astype(F32)
    quarter = HEAD_DIM // 4
    inv_freq = ROPE_BASE ** (-2.0 * jnp.arange(quarter, dtype=F32) / (HEAD_DIM // 2))
    ang_r = row[:, None] * inv_freq
    ang_c = col[:, None] * inv_freq
    cr, sr, cc, sc = jnp.cos(ang_r), jnp.sin(ang_r), jnp.cos(ang_c), jnp.sin(ang_c)
    cos = jnp.concatenate([cr, cr, cc, cc], axis=-1)
    sin = jnp.concatenate([-sr, sr, -sc, sc], axis=-1)
    cos = jnp.concatenate([jnp.ones((ctx_len, HEAD_DIM), F32), cos], axis=0)
    sin = jnp.concatenate([jnp.zeros((ctx_len, HEAD_DIM), F32), sin], axis=0)
    reps = QK_COLS // HEAD_DIM
    return jnp.tile(cos, (1, reps)), jnp.tile(sin, (1, reps))


def _swap_cols():
    q = HEAD_DIM // 4
    one = np.concatenate([np.arange(q, 2 * q), np.arange(0, q), np.arange(3 * q, 4 * q), np.arange(2 * q, 3 * q)])
    return np.concatenate([one + HEAD_DIM * i for i in range(QK_COLS // HEAD_DIM)])


def kernel(x, c, ctx, c_ctx, ada_w, ada_b, w_in, rwkv_conv, rwkv_w0, rwkv_w2, rwkv_a0, rwkv_a2, rwkv_g2, rwkv_k_k, rwkv_k_a, rwkv_r_k, rwkv_ln_w, rwkv_ln_b, diff_lambda, diff_subln_w, hgrn_lb_logits, hgrn_norm_w, w_out, ln1_w, ln1_b, peer_wq, peer_subkeys, peer_u, peer_v, ln2_w, ln2_b):
    bsz, seq, d = x.shape
    ctx_len = ctx.shape[1]
    depth = w_in.shape[0]
    t_len = ctx_len + seq
    tm = 256 if ctx_len % 256 == 0 else 128
    assert ctx_len % tm == 0 and seq % tm == 0 and seq % GRID_W == 0
    ctx_blocks = ctx_len // tm
    alpha = (2.0 * depth) ** 0.25

    consts = _scan_consts()
    hsum = jnp.asarray((np.arange(A_WIDTH)[:, None] // HEAD_DIM == np.arange(A_WIDTH)[None, :] // HEAD_DIM)
                       .astype(np.float32))
    cos, sin = _rope_tables(seq // GRID_W, ctx_len)
    swap = _swap_cols()

    lb_p = jax.nn.softmax(hgrn_lb_logits.astype(F32), axis=0)
    lower_bounds = jnp.cumsum(lb_p, axis=0) - lb_p[0]

    n_cond = 8 * ((bsz + 1 + 7) // 8)
    cc = jnp.zeros((n_cond, d), F32).at[0].set(c_ctx).at[1:1 + bsz].set(c)
    mods = _ada_call(cc, ada_w, ada_b)

    xs = jnp.concatenate([ctx, x], axis=1)
    for l in range(depth):
        last = l == depth - 1
        m = mods[l].reshape(n_cond, 6, d)
        mod = jnp.stack([jnp.broadcast_to(m[0], (bsz, 6, d)), m[1:1 + bsz]], axis=1)
        q_cols = w_in[l][:, A_COLS:A_COLS + QK_COLS]
        k_cols = w_in[l][:, A_COLS + QK_COLS:A_COLS + 2 * QK_COLS]
        w_ext = jnp.concatenate([w_in[l], q_cols[:, swap], k_cols[:, swap]], axis=1).astype(BF16)
        P = dict(rwkv_conv=rwkv_conv[l], rwkv_w0=rwkv_w0[l], rwkv_w2=rwkv_w2[l], rwkv_a0=rwkv_a0[l],
                 rwkv_a2=rwkv_a2[l], rwkv_g2=rwkv_g2[l], rwkv_k_k=rwkv_k_k[l], rwkv_k_a=rwkv_k_a[l],
                 rwkv_r_k=rwkv_r_k[l], rwkv_ln_w=rwkv_ln_w[l], rwkv_ln_b=rwkv_ln_b[l],
                 hgrn_norm_w=hgrn_norm_w[l], w_out_bf16=w_out[l].astype(BF16),
                 ln1_w=ln1_w[l], ln1_b=ln1_b[l], peer_wq_bf16=peer_wq[l].astype(BF16))

        oa, q, k, v, oc = _inproj_call(xs, mod, w_ext, cos, sin, tm, ctx_blocks)
        r, k2, vv, avec, bvec, lw, g, bonus = _rwkv_prep_call(oa, P, hsum, tm, ctx_blocks)
        rw_o = _rwkv_scan_call(r, k2, vv, avec, bvec, lw, consts, ctx_len // CHUNK)
        hg_o = _hgrn_scan_call(oc, lower_bounds[l], consts, ctx_len // CHUNK)
        lam_init = 0.8 - 0.6 * math.exp(-0.3 * l)
        row0 = ctx_blocks if last else 0
        t_out = t_len - row0 * tm
        yb = _attn_call(q, k, v, diff_lambda[l], diff_subln_w[l], lam_init, tm, ctx_len, skip_ctx=last)
        x1, u2, qp = _outproj_call(xs, mod, rw_o, g, bonus, yb, hg_o, oc, hsum, P, alpha, tm, ctx_blocks, row0)

        x2 = _peer_ffn(qp.reshape(bsz * t_out, -1), u2.reshape(bsz * t_out, d), x1.reshape(bsz * t_out, d), mod,
                       peer_subkeys[l], peer_u[l], peer_v[l], ln2_w[l], ln2_b[l], alpha, t_out,
                       ctx_len - row0 * tm)
        xs = x2.reshape(bsz, t_out, d)
    return xs
```

```python
import functools
import math

import numpy as np
import jax
import jax.numpy as jnp
from jax import lax
from jax.experimental import pallas as pl
from jax.experimental.pallas import tpu as pltpu

F32 = jnp.float32
BF16 = jnp.bfloat16
HIGHEST = lax.Precision.HIGHEST

D_MODEL = 1024
GRID_W = 64
HEAD_DIM = 64
A_WIDTH = 256
A_DECAY_LORA = 32
A_ICLR_LORA = 32
A_GATE_LORA = 64
B_WIDTH = 512
B_HEADS = 4
B_V_DIM = 128
C_WIDTH = 256
ROPE_BASE = 10000.0
MIN_FORGET = 1e-30
PEER_HEADS = 8
PEER_NKEYS = 128
PEER_TOPK = 16
PEER_QDIM = 256
LN_EPS = 1e-5
RWKV_GN_EPS = 64e-5

A_COLS = 3 * A_WIDTH + A_DECAY_LORA + A_ICLR_LORA + A_GATE_LORA
QK_COLS = 2 * B_HEADS * HEAD_DIM
C_COLS = 5 * C_WIDTH
IN_WIDTH = A_COLS + 3 * QK_COLS + C_COLS
SCAN_HEADS = 4
CHUNK = 64
STACK = SCAN_HEADS * CHUNK
SCAN_BATCH = 2
LANES = 128
VMEM_LIMIT = 56 * 1024 * 1024


def _cparams(n_axes, vmem=None):
    return pltpu.CompilerParams(dimension_semantics=("arbitrary",) * n_axes,
                                vmem_limit_bytes=vmem or VMEM_LIMIT)


def _ln(x):
    xc = x - jnp.mean(x, axis=-1, keepdims=True)
    return xc * lax.rsqrt(jnp.mean(xc * xc, axis=-1, keepdims=True) + LN_EPS)


def _mm(a, b, dims=((1,), (0,)), exact=False):
    dn = (dims, ((), ()))
    if exact is True:
        return lax.dot_general(a, b, dn, precision=HIGHEST, preferred_element_type=F32)
    dot = lambda p, q: lax.dot_general(p, q, dn, preferred_element_type=F32)
    a_hi = a.astype(BF16)
    b_hi = b.astype(BF16)
    if exact == "rhs3":
        r1 = b - b_hi.astype(F32)
        b_mid = r1.astype(BF16)
        b_lo = (r1 - b_mid.astype(F32)).astype(BF16)
        return dot(a_hi, b_hi) + (dot(a_hi, b_mid) + dot(a_hi, b_lo))
    if exact == "lhs3":
        r1 = a - a_hi.astype(F32)
        a_mid = r1.astype(BF16)
        a_lo = (r1 - a_mid.astype(F32)).astype(BF16)
        return dot(a_hi, b_hi) + (dot(a_mid, b_hi) + dot(a_lo, b_hi))
    if exact == "x3":
        a_lo = (a - a_hi.astype(F32)).astype(BF16)
        b_lo = (b - b_hi.astype(F32)).astype(BF16)
        return dot(a_hi, b_hi) + (dot(a_hi, b_lo) + dot(a_lo, b_hi))
    return dot(a_hi, b_hi)


NT = ((1,), (1,))
TN = ((0,), (0,))


def _ada_kernel(c_ref, w_ref, b_ref, o_ref):
    c = c_ref[...]
    s = c * jax.nn.sigmoid(c)
    o_ref[...] = _mm(s, w_ref[...], exact=True) + b_ref[...]


def _ada_call(cc, ada_w, ada_b):
    depth, d, n = ada_w.shape
    bn = 512
    return pl.pallas_call(
        _ada_kernel,
        grid=(depth, n // bn),
        in_specs=[pl.BlockSpec(cc.shape, lambda l, j: (0, 0)),
                  pl.BlockSpec((None, d, bn), lambda l, j: (l, 0, j)),
                  pl.BlockSpec((None, 1, bn), lambda l, j: (l, 0, j))],
        out_specs=pl.BlockSpec((None, cc.shape[0], bn), lambda l, j: (l, 0, j)),
        out_shape=jax.ShapeDtypeStruct((depth, cc.shape[0], n), F32),
        compiler_params=_cparams(2),
        name="ada_mod",
    )(cc, ada_w, ada_b.reshape(depth, 1, n))


def _inproj_kernel(x_ref, mod_ref, w_ref, cos_ref, sin_ref, oa_ref, q_ref, k_ref, v_ref, oc_ref):
    m = mod_ref[0, 0]
    u = _ln(x_ref[0]) * (1.0 + m[1:2]) + m[0:1]
    h = _mm(u, w_ref[...])
    oa_ref[0] = h[:, 0:A_COLS]
    cos = cos_ref[...]
    sin = sin_ref[...]
    q0 = A_COLS
    k0 = q0 + QK_COLS
    v0 = k0 + QK_COLS
    c0 = v0 + QK_COLS
    qs0 = IN_WIDTH
    ks0 = qs0 + QK_COLS
    q = (h[:, q0:k0] * cos + h[:, qs0:ks0] * sin) * (math.log2(math.e) * HEAD_DIM ** -0.5)
    k = h[:, k0:v0] * cos + h[:, ks0:ks0 + QK_COLS] * sin
    q_ref[0] = q.astype(BF16)
    k_ref[0] = k.astype(BF16)
    v_ref[0] = h[:, v0:c0].astype(BF16)
    oc_ref[0] = h[:, c0:IN_WIDTH]


def _inproj_call(x, mod, w_ext, cos, sin, tm, ctx_blocks):
    b, t, d = x.shape
    n_ext = w_ext.shape[1]
    tok = lambda width: pl.BlockSpec((1, tm, width), lambda i, j: (i, j, 0))
    return pl.pallas_call(
        _inproj_kernel,
        grid=(b, t // tm),
        in_specs=[tok(d),
                  pl.BlockSpec((1, 1, 6, d), lambda i, j: (i, (j >= ctx_blocks).astype(jnp.int32), 0, 0)),
                  pl.BlockSpec((d, n_ext), lambda i, j: (0, 0)),
                  pl.BlockSpec((tm, QK_COLS), lambda i, j: (j, 0)),
                  pl.BlockSpec((tm, QK_COLS), lambda i, j: (j, 0))],
        out_specs=[tok(A_COLS), tok(QK_COLS), tok(QK_COLS), tok(QK_COLS), tok(C_COLS)],
        out_shape=[jax.ShapeDtypeStruct((b, t, A_COLS), F32),
                   jax.ShapeDtypeStruct((b, t, QK_COLS), BF16),
                   jax.ShapeDtypeStruct((b, t, QK_COLS), BF16),
                   jax.ShapeDtypeStruct((b, t, QK_COLS), BF16),
                   jax.ShapeDtypeStruct((b, t, C_COLS), F32)],
        compiler_params=_cparams(2),
        name="in_proj",
    )(x, mod, w_ext, cos, sin)


def _rwkv_prep_kernel(cur_ref, prev_ref, next_ref, conv_ref, w0_ref, w2_ref, a0_ref, a2_ref, g2_ref,
                      kk_ref, ka_ref, rk_ref, hsum_ref,
                      r_o, k_o, v_o, a_o, b_o, lw_o, g_o, bonus_o, *, ctx_blocks, n_blocks):
    j = pl.program_id(1)
    cur = cur_ref[0]
    tm = cur.shape[0]
    w3 = 3 * A_WIDTH
    rkv = cur[:, 0:w3]
    has_prev = jnp.logical_and(j != 0, j != ctx_blocks)
    has_next = jnp.logical_and(j != ctx_blocks - 1, j != n_blocks - 1)
    prev_row = jnp.where(has_prev, prev_ref[0][7:8, 0:w3], 0.0)
    next_row = jnp.where(has_next, next_ref[0][0:1, 0:w3], 0.0)
    row = lax.broadcasted_iota(jnp.int32, (tm, w3), 0)
    xm1 = jnp.where(row == 0, prev_row, pltpu.roll(rkv, 1, 0))
    xp1 = jnp.where(row == tm - 1, next_row, pltpu.roll(rkv, tm - 1, 0))
    cw = conv_ref[...]
    conv = cw[0:1] * xm1 + cw[1:2] * rkv + cw[2:3] * xp1
    r = conv[:, 0:A_WIDTH]
    k = conv[:, A_WIDTH:2 * A_WIDTH]
    v = conv[:, 2 * A_WIDTH:w3]
    o = w3
    w_lo = cur[:, o:o + A_DECAY_LORA]
    a_lo = cur[:, o + A_DECAY_LORA:o + A_DECAY_LORA + A_ICLR_LORA]
    g_lo = cur[:, o + A_DECAY_LORA + A_ICLR_LORA:A_COLS]
    a = jax.nn.sigmoid(a0_ref[...] + _mm(a_lo, a2_ref[...], exact=True))
    g = _mm(jax.nn.sigmoid(g_lo), g2_ref[...], exact=True)
    hsum = hsum_ref[...]
    kk = k * kk_ref[...]
    ss = _mm(kk * kk, hsum, exact="lhs3")
    kk = kk * lax.rsqrt(jnp.maximum(ss, 1e-24))
    k2 = k * (1.0 + (a - 1.0) * ka_ref[...])
    tw = jnp.tanh(w_lo)
    w0 = w0_ref[...]
    for d in range(2):
        wl = w0[d:d + 1] + _mm(tw, w2_ref[d], exact=True)
        lw_o[d, 0] = -math.exp(-0.5) * jax.nn.sigmoid(wl)
    r_o[0] = r
    k_o[0] = k2
    v_o[0] = v
    a_o[0] = -kk
    b_o[0] = kk * a
    g_o[0] = g
    bonus_o[0] = _mm(r * k2 * rk_ref[...], hsum, exact="lhs3") * v


def _rwkv_prep_call(oa, P, hsum, tm, ctx_blocks):
    b, t, _ = oa.shape
    n_blocks = t // tm
    per8 = tm // 8
    last8 = t // 8 - 1
    full = lambda arr: pl.BlockSpec(arr.shape, lambda i, j: (0,) * arr.ndim)
    tok = pl.BlockSpec((1, tm, A_WIDTH), lambda i, j: (i, j, 0))
    row = lambda p: p.reshape(1, -1)
    params = [P['rwkv_conv'], P['rwkv_w0'], P['rwkv_w2'], row(P['rwkv_a0']), P['rwkv_a2'], P['rwkv_g2'],
              row(P['rwkv_k_k']), row(P['rwkv_k_a']), row(P['rwkv_r_k']), hsum]
    shp = jax.ShapeDtypeStruct((b, t, A_WIDTH), F32)
    return pl.pallas_call(
        functools.partial(_rwkv_prep_kernel, ctx_blocks=ctx_blocks, n_blocks=n_blocks),
        grid=(b, n_blocks),
        in_specs=[pl.BlockSpec((1, tm, A_COLS), lambda i, j: (i, j, 0)),
                  pl.BlockSpec((1, 8, A_COLS), lambda i, j: (i, jnp.maximum(j * per8 - 1, 0), 0)),
                  pl.BlockSpec((1, 8, A_COLS), lambda i, j: (i, jnp.minimum((j + 1) * per8, last8), 0))]
                 + [full(p) for p in params],
        out_specs=[tok, tok, tok, tok, tok,
                   pl.BlockSpec((2, 1, tm, A_WIDTH), lambda i, j: (0, i, j, 0)), tok, tok],
        out_shape=[shp, shp, shp, shp, shp, jax.ShapeDtypeStruct((2, b, t, A_WIDTH), F32), shp, shp],
        compiler_params=_cparams(2),
        name="rwkv_prep",
    )(oa, oa, oa, *params)


def _scan_consts():
    c = CHUNK
    t = np.arange(c)
    tri = np.zeros((2, c, c), np.float32)
    tri[0] = (t[None, :] <= t[:, None])
    tri[1] = (t[None, :] >= t[:, None])
    st = np.arange(STACK)
    same_head = (st[:, None] // c) == (st[None, :] // c)
    tt = st[:, None] % c
    ss = st[None, :] % c
    strict = np.stack([same_head & (ss < tt), same_head & (ss > tt)]).astype(np.float32)
    incl = np.stack([same_head & (ss <= tt), same_head & (ss >= tt)]).astype(np.float32)
    head_mask = ((st[:, None] // c) == (np.arange(SCAN_HEADS * HEAD_DIM)[None, :] // HEAD_DIM)).astype(np.float32)
    n_lv = int(math.log2(c))
    seg = np.zeros((2, (n_lv + 1) * c, c), np.float32)
    seg[0, :c] = tri[0]
    seg[1, :c] = tri[1]
    level = -np.ones((2, STACK, STACK), np.int32)
    for d in range(2):
        level[d][same_head & (tt == ss)] = 0
    for l in range(1, n_lv + 1):
        n = 2 ** l
        for ti in range(c):
            s0 = (ti // n) * n
            m = s0 + n // 2 - 1
            hh = s0 + n // 2
            if ti > m:
                seg[0, l * c + ti, m + 1:ti + 1] = 1.0
            else:
                seg[0, l * c + ti, ti + 1:m + 1] = 1.0
            if ti < hh:
                seg[1, l * c + ti, ti:hh] = 1.0
            else:
                seg[1, l * c + ti, hh:ti] = 1.0
        same_blk = (tt // n) == (ss // n)
        t_second = (tt % n) >= n // 2
        s_second = (ss % n) >= n // 2
        level[0][same_head & same_blk & t_second & ~s_second] = l
        level[1][same_head & same_blk & ~t_second & s_second] = l
    return dict(tri=jnp.asarray(tri), strict=jnp.asarray(strict), incl=jnp.asarray(incl),
                head_mask=jnp.asarray(head_mask), seg=jnp.asarray(seg), level=jnp.asarray(level))


def _tile4(x):
    return jnp.concatenate([x] * SCAN_HEADS, axis=0)


def _fold4(x):
    c = CHUNK
    return x[0:c] + x[c:2 * c] + x[2 * c:3 * c] + x[3 * c:4 * c]


def _chunk_index(d, j, nc_ctx, nc):
    if d == 0:
        return j
    return jnp.where(j < nc_ctx, nc_ctx - 1 - j, nc - 1 - (j - nc_ctx))


def _rwkv_chunk(tri, strict, incl, hm, r, k, v, a, b, lw, z, mm, mmi):
    cum = _mm(tri, lw, exact="rhs3")
    tot = jnp.sum(lw, axis=0, keepdims=True)
    e_neg = jnp.exp(-cum)
    e_end = jnp.exp(tot - cum)
    a_sm = _tile4(a * jnp.exp(cum - lw)) * hm
    r_sm = _tile4(r * jnp.exp(cum)) * hm
    v_sm = _tile4(v) * hm
    b_rep = _tile4(b * e_neg)
    k_rep = _tile4(k * e_neg)
    zero = jnp.zeros((STACK, STACK), F32)
    n_ab = jnp.where(strict, mm(a_sm, b_rep, NT), zero)
    n_ak = jnp.where(strict, mm(a_sm, k_rep, NT), zero)
    n_rb = jnp.where(incl, mm(r_sm, b_rep, NT), zero)
    n_rk = jnp.where(incl, mm(r_sm, k_rep, NT), zero)
    rows = lax.broadcasted_iota(jnp.int32, (STACK, STACK), 0)
    cols = lax.broadcasted_iota(jnp.int32, (STACK, STACK), 1)
    inv = jnp.where(rows == cols, 1.0, 0.0) + n_ab
    npow = n_ab
    for _ in range(int(math.log2(CHUNK)) - 1):
        npow = mmi(npow, npow)
        inv = inv + mmi(inv, npow)
    u_sm = mm(inv, mm(a_sm, z, NT) + mm(n_ak, v_sm))
    o_sm = mm(r_sm, z, NT) + mm(n_rb, u_sm) + mm(n_rk, v_sm)
    z_new = z * jnp.exp(tot) + mm(u_sm, _tile4(b * e_end) * hm, TN) + mm(v_sm, _tile4(k * e_end) * hm, TN)
    return _fold4(o_sm), z_new


def _rwkv_scan_kernel(tri_ref, strict_ref, incl_ref, hm_ref, *refs, exact, inv_exact):
    ins, (of_ref, ob_ref, z_ref) = refs[:12], refs[12:]

    @pl.when(pl.program_id(1) == 0)
    def _():
        z_ref[...] = jnp.zeros_like(z_ref)

    mm = functools.partial(_mm, exact=exact)
    mmi = functools.partial(_mm, exact=inv_exact)
    hm = hm_ref[...]
    for bb in range(SCAN_BATCH):
        for d, o_ref in enumerate((of_ref, ob_ref)):
            r, k, v, a, b = (ref[bb] for ref in ins[6 * d:6 * d + 5])
            o, z_new = _rwkv_chunk(tri_ref[d], strict_ref[d] > 0.0, incl_ref[d] > 0.0, hm, r, k, v, a, b,
                                   ins[6 * d + 5][0, bb], z_ref[2 * bb + d], mm, mmi)
            o_ref[bb] = o
            z_ref[2 * bb + d] = z_new


def _rwkv_scan_call(r, k, v, a, bvec, lw, consts, nc_ctx, exact=False, inv_exact=False):
    b, t, w = r.shape
    nc = t // CHUNK
    full = lambda arr: pl.BlockSpec(arr.shape, lambda i, j: (0,) * arr.ndim)
    specs, args = [], []
    for d in range(2):
        cidx = functools.partial(_chunk_index, d, nc_ctx=nc_ctx, nc=nc)
        tok = pl.BlockSpec((SCAN_BATCH, CHUNK, w), lambda i, j, cidx=cidx: (i, cidx(j), 0))
        specs += [tok] * 5 + [pl.BlockSpec((1, SCAN_BATCH, CHUNK, w),
                                           lambda i, j, cidx=cidx, d=d: (d, i, cidx(j), 0))]
        args += [r, k, v, a, bvec, lw]
    out_specs = [pl.BlockSpec((SCAN_BATCH, CHUNK, w), lambda i, j, d=d: (i, _chunk_index(d, j, nc_ctx, nc), 0))
                 for d in range(2)]
    cs = [consts['tri'], consts['strict'], consts['incl'], consts['head_mask']]
    return pl.pallas_call(
        functools.partial(_rwkv_scan_kernel, exact=exact, inv_exact=inv_exact),
        grid=(b // SCAN_BATCH, nc),
        in_specs=[full(c) for c in cs] + specs,
        out_specs=out_specs,
        out_shape=[jax.ShapeDtypeStruct((b, t, w), F32)] * 2,
        scratch_shapes=[pltpu.VMEM((2 * SCAN_BATCH, STACK, STACK), F32)],
        compiler_params=_cparams(2),
        name="rwkv_scan",
    )(*cs, *args)


def _hgrn_chunk(seg, level, hm, lb, qr, zf, v, z, mm):
    q = qr * jax.nn.sigmoid(qr)
    f = lb + (1.0 - lb) * jax.nn.sigmoid(zf)
    logf = jnp.log(jnp.maximum(f, MIN_FORGET))
    k = (1.0 - lb) * jax.nn.sigmoid(-zf)
    segs = _mm(seg, logf, exact="rhs3")
    cum = segs[0:CHUNK]
    tot = jnp.sum(logf, axis=0, keepdims=True)
    attn = jnp.where(level == 0, mm(_tile4(q) * hm, _tile4(k), NT), 0.0)
    for l in range(1, int(math.log2(CHUNK)) + 1):
        e = jnp.exp(segs[l * CHUNK:(l + 1) * CHUNK])
        attn = jnp.where(level == l, mm(_tile4(q * e) * hm, _tile4(k * e), NT), attn)
    v_sm = _tile4(v) * hm
    o_sm = mm(_tile4(q * jnp.exp(cum)) * hm, z, NT) + mm(attn, v_sm)
    z_new = z * jnp.exp(tot) + mm(v_sm, _tile4(k * jnp.exp(tot - cum)) * hm, TN)
    return _fold4(o_sm), z_new


def _hgrn_scan_kernel(seg_ref, level_ref, hm_ref, lb_ref, *refs, exact):
    ins, (of_ref, ob_ref, z_ref) = refs[:6], refs[6:]

    @pl.when(pl.program_id(1) == 0)
    def _():
        z_ref[...] = jnp.zeros_like(z_ref)

    mm = functools.partial(_mm, exact=exact)
    hm = hm_ref[...]
    lb = lb_ref[...]
    for bb in range(SCAN_BATCH):
        for d, o_ref in enumerate((of_ref, ob_ref)):
            qr, zf, v = (ref[bb] for ref in ins[3 * d:3 * d + 3])
            o, z_new = _hgrn_chunk(seg_ref[d], level_ref[d], hm, lb, qr, zf, v, z_ref[2 * bb + d], mm)
            o_ref[bb] = o
            z_ref[2 * bb + d] = z_new


def _hgrn_scan_call(oc, lb, consts, nc_ctx, exact=False):
    b, t, _ = oc.shape
    w = C_WIDTH
    nc = t // CHUNK
    full = lambda arr: pl.BlockSpec(arr.shape, lambda i, j: (0,) * arr.ndim)
    specs = []
    for d in range(2):
        cidx = functools.partial(_chunk_index, d, nc_ctx=nc_ctx, nc=nc)
        specs += [pl.BlockSpec((SCAN_BATCH, CHUNK, w), lambda i, j, cidx=cidx, col=col: (i, cidx(j), col))
                  for col in (0, 1 + d, 3)]
    out_specs = [pl.BlockSpec((SCAN_BATCH, CHUNK, w), lambda i, j, d=d: (i, _chunk_index(d, j, nc_ctx, nc), 0))
                 for d in range(2)]
    cs = [consts['seg'], consts['level'], consts['head_mask'], lb.reshape(1, w)]
    return pl.pallas_call(
        functools.partial(_hgrn_scan_kernel, exact=exact),
        grid=(b // SCAN_BATCH, nc),
        in_specs=[full(c) for c in cs] + specs,
        out_specs=out_specs,
        out_shape=[jax.ShapeDtypeStruct((b, t, w), F32)] * 2,
        scratch_shapes=[pltpu.VMEM((2 * SCAN_BATCH, STACK, STACK), F32)],
        compiler_params=_cparams(2),
        name="hgrn_scan",
    )(*cs, *([oc] * 6))


def _attn_kernel(lam_ref, w_ref, q_ref, k_ref, v_ref, o_ref, *, lam_init, ctx_len, ctx_blocks):
    j = pl.program_id(2)
    ll = lam_ref[...]
    lam = (jnp.exp(jnp.sum(ll[0:1] * ll[1:2], axis=-1, keepdims=True))
           - jnp.exp(jnp.sum(ll[2:3] * ll[3:4], axis=-1, keepdims=True)) + lam_init)
    q = q_ref[0]
    first = lax.broadcasted_iota(jnp.int32, q.shape, 1) < HEAD_DIM
    zq = jnp.zeros_like(q)
    q1 = jnp.where(first, q, zq)
    q2 = jnp.where(first, zq, q)

    def attend(k, v):
        def softmax_parts(qm):
            s = lax.dot_general(qm, k, (NT, ((), ())), preferred_element_type=F32)
            p = jnp.exp2(s - jnp.max(s, axis=-1, keepdims=True))
            return p, 1.0 / jnp.sum(p, axis=-1, keepdims=True)
        p1, i1 = softmax_parts(q1)
        p2, i2 = softmax_parts(q2)
        a = p1 * i1 - p2 * (lam * i2)
        o = jnp.dot(a.astype(BF16), v, preferred_element_type=F32)
        y = o * lax.rsqrt(jnp.mean(o * o, axis=-1, keepdims=True) + LN_EPS)
        o_ref[0] = y * w_ref[...] * (1.0 - lam_init)

    @pl.when(j < ctx_blocks)
    def _():
        attend(k_ref[0, 0:ctx_len], v_ref[0, 0:ctx_len])

    @pl.when(j >= ctx_blocks)
    def _():
        attend(k_ref[0], v_ref[0])


def _attn_call(q, k, v, diff_lambda, subln_w, lam_init, tq, ctx_len, skip_ctx):
    b, t, _ = q.shape
    ctx_blocks = ctx_len // tq
    j0 = ctx_blocks if skip_ctx else 0
    kv = pl.BlockSpec((1, t, B_V_DIM), lambda i, h, j: (i, 0, h))
    qo = pl.BlockSpec((1, tq, B_V_DIM), lambda i, h, j: (i, j + j0, h))
    return pl.pallas_call(
        functools.partial(_attn_kernel, lam_init=lam_init, ctx_len=ctx_len, ctx_blocks=ctx_blocks - j0),
        grid=(b, B_HEADS, t // tq - j0),
        in_specs=[pl.BlockSpec((4, HEAD_DIM), lambda i, h, j: (0, 0)),
                  pl.BlockSpec((1, B_V_DIM), lambda i, h, j: (0, 0)),
                  qo, kv, kv],
        out_specs=qo,
        out_shape=jax.ShapeDtypeStruct((b, t, B_WIDTH), F32),
        compiler_params=_cparams(3),
        name="diff_attn",
    )(diff_lambda, subln_w.reshape(1, B_V_DIM), q, k, v)


def _outproj_kernel(x_ref, mod_ref, of_ref, ob_ref, g_ref, bonus_ref, yb_ref, cf_ref, cb_ref, cg_ref,
                    hsum_ref, lnw_ref, lnb_ref, hw_ref, wout_ref, l1w_ref, l1b_ref, wq_ref,
                    x1_ref, u2_ref, qp_ref, *, alpha):
    m = mod_ref[0, 0]
    hsum = hsum_ref[...]
    inv = 1.0 / HEAD_DIM
    o = of_ref[0] + ob_ref[0]
    oc = o - _mm(o, hsum, exact="lhs3") * inv
    ya = oc * lax.rsqrt(_mm(oc * oc, hsum, exact="lhs3") * inv + RWKV_GN_EPS)
    ya = (ya * lnw_ref[...] + lnb_ref[...] + bonus_ref[0]) * g_ref[0]
    c = cf_ref[0] + cb_ref[0]
    cg = cg_ref[0]
    yc = c * lax.rsqrt(_mm(c * c, hsum, exact="lhs3") * inv + LN_EPS) * hw_ref[...] * (cg * jax.nn.sigmoid(cg))
    y = jnp.concatenate([ya, yb_ref[0], yc], axis=-1)
    proj = _mm(y, wout_ref[...])
    x1 = _ln(alpha * x_ref[0] + m[2:3] * proj) * l1w_ref[...] + l1b_ref[...]
    x1_ref[0] = x1
    u2 = _ln(x1) * (1.0 + m[4:5]) + m[3:4]
    u2_ref[0] = u2
    qp_ref[0] = _mm(u2, wq_ref[...])


def _outproj_call(x, mod, rw_o, rw_g, rw_bonus, yb, hg_o, oc, hsum, P, alpha, tm, ctx_blocks, row0):
    b, t, d = x.shape
    nq = P['peer_wq_bf16'].shape[1]
    t_out = t - row0 * tm
    tok = lambda width, col=0: pl.BlockSpec((1, tm, width), lambda i, j: (i, j + row0, col))
    otok = lambda width: pl.BlockSpec((1, tm, width), lambda i, j: (i, j, 0))
    full = lambda arr: pl.BlockSpec(arr.shape, lambda i, j: (0,) * arr.ndim)
    row = lambda p: p.reshape(1, -1)
    params = [hsum, row(P['rwkv_ln_w']), row(P['rwkv_ln_b']), row(jnp.tile(P['hgrn_norm_w'], SCAN_HEADS)),
              P['w_out_bf16'], row(P['ln1_w']), row(P['ln1_b']), P['peer_wq_bf16']]
    return pl.pallas_call(
        functools.partial(_outproj_kernel, alpha=alpha),
        grid=(b, t_out // tm),
        in_specs=[tok(d),
                  pl.BlockSpec((1, 1, 6, d), lambda i, j: (i, (j + row0 >= ctx_blocks).astype(jnp.int32), 0, 0)),
                  tok(A_WIDTH), tok(A_WIDTH), tok(A_WIDTH), tok(A_WIDTH), tok(B_WIDTH),
                  tok(C_WIDTH), tok(C_WIDTH), tok(C_WIDTH, 4)] + [full(p) for p in params],
        out_specs=[otok(d), otok(d), otok(nq)],
        out_shape=[jax.ShapeDtypeStruct((b, t_out, d), F32), jax.ShapeDtypeStruct((b, t_out, d), F32),
                   jax.ShapeDtypeStruct((b, t_out, nq), F32)],
        compiler_params=_cparams(2),
        name="out_proj",
    )(x, mod, rw_o[0], rw_o[1], rw_g, rw_bonus, yb, hg_o[0], hg_o[1], oc, *params)


def _topk_rows(s, order=None, payload=None):
    if order is None:
        order = lax.broadcasted_iota(jnp.int32, s.shape, 0)
    big = jnp.int32(2 ** 30)
    vals, picks = [], []
    for _ in range(PEER_TOPK):
        m = jnp.max(s, axis=0, keepdims=True)
        idx = jnp.min(jnp.where(s == m, order, big), axis=0, keepdims=True)
        hit = order == idx
        vals.append(m)
        picks.append(idx if payload is None else jnp.max(jnp.where(hit, payload, -1), axis=0, keepdims=True))
        s = jnp.where(hit, -jnp.inf, s)
    return jnp.concatenate(vals, axis=0), jnp.concatenate(picks, axis=0)


def _pair_candidates(sv, si):
    k = PEER_TOPK
    vals, flat, eid = [], [], []
    tokens = sv[0].shape[1]
    row8 = lax.broadcasted_iota(jnp.int32, (8, tokens), 0)
    row16 = lax.broadcasted_iota(jnp.int32, (k, tokens), 0)
    neg = -jnp.inf
    for i in range(8):
        n_valid = k // (i + 1)
        rows, row = (k, row16) if n_valid > 8 else (8, row8)
        v = sv[0][i:i + 1] + sv[1][0:rows]
        vals.append(v if n_valid == rows else jnp.where(row < n_valid, v, neg))
        flat.append(i * k + row)
        eid.append(si[0][i:i + 1] * PEER_NKEYS + si[1][0:rows])
    vals.append(sv[0][8:k] + sv[1][0:1])
    flat.append((row8 + 8) * k)
    eid.append(si[0][8:k] * PEER_NKEYS + si[1][0:1])
    cat = lambda xs: jnp.concatenate(xs, axis=0)
    return cat(vals), cat(flat), cat(eid)


def _route_head(q_ref, keys_ref, e_ref, g_ref, h):
    half = PEER_QDIM // 2
    sv, si = [], []
    for p in range(2):
        start = pl.multiple_of((2 * h + p) * half, half)
        qhp = q_ref[:, pl.ds(start, half)]
        s = _mm(keys_ref[p], qhp, NT, exact=True)
        vals, idx = _topk_rows(s)
        sv.append(vals)
        si.append(idx)
    cand, flat, eid = _pair_candidates(sv, si)
    top, e = _topk_rows(cand, flat, eid)
    ex = jnp.exp(top - top[0:1])
    gate = ex / jnp.sum(ex, axis=0, keepdims=True)
    off = pl.multiple_of(h * PEER_TOPK, PEER_TOPK)
    e_ref[0, pl.ds(off, PEER_TOPK), :] = e
    g_ref[0, pl.ds(off, PEER_TOPK), :] = gate


def _peer_route_kernel(q_ref, keys_ref, e_ref, g_ref):
    def head(h, carry):
        _route_head(q_ref, keys_ref, e_ref, g_ref, h)
        return carry

    lax.fori_loop(0, PEER_HEADS, head, 0)


def _peer_route_call(qp, subkeys, tb, n_part):
    nq = qp.shape[1]
    nblk = n_part // tb
    nsel = PEER_HEADS * PEER_TOPK
    out = pl.BlockSpec((1, nsel, tb), lambda i: (i, 0, 0))
    return pl.pallas_call(
        _peer_route_kernel,
        grid=(nblk,),
        in_specs=[pl.BlockSpec((tb, nq), lambda i: (i, 0)),
                  pl.BlockSpec(subkeys.shape, lambda i: (0, 0, 0))],
        out_specs=[out, out],
        out_shape=[jax.ShapeDtypeStruct((nblk, nsel, tb), jnp.int32),
                   jax.ShapeDtypeStruct((nblk, nsel, tb), F32)],
        compiler_params=_cparams(1),
        name="peer_route",
    )(qp, subkeys)


PACK_ROWS = 4
NSEL = PEER_HEADS * PEER_TOPK


def _pack_table(tab):
    n, d = tab.shape
    bits = lax.bitcast_convert_type(tab.astype(BF16), jnp.uint16).astype(jnp.uint32)
    word = bits[:, :d // 2] | (bits[:, d // 2:] << 16)
    return lax.bitcast_convert_type(word, jnp.int32).reshape(n * PACK_ROWS, LANES)


PEER_PARTS = 8
TOK_GROUP = 16

def _gather_group(idx_ref, tab_ref, tile_ref, t0):
    tok_idx = [idx_ref.at[pl.ds(pl.multiple_of((t0 + u) * NSEL, NSEL), NSEL)] for u in range(TOK_GROUP)]
    for mi in range(NSEL):
        for u in range(TOK_GROUP):
            i = pl.multiple_of(tok_idx[u][mi], PACK_ROWS)
            tile_ref[u, pl.ds(PACK_ROWS * mi, PACK_ROWS), :] = tab_ref[pl.ds(i, PACK_ROWS), :]


def _packed_rows(tile_ref, u):
    g = jnp.concatenate([tile_ref[u, pl.ds(c, NSEL, stride=PACK_ROWS), :] for c in range(PACK_ROWS)], axis=1)
    return pltpu.bitcast(g, BF16)


def _split_hi_lo(x):
    hi = x.astype(BF16).astype(F32)
    return hi, x - hi


ROUTE_TB = 128


def _peer_act_kernel(idx_ref, u_ref, gate_ref, tab_ref, *refs, fuse_route):
    if fuse_route:
        q_ref, keys_ref, w_ref, e_ref, g_ref, tile_ref, act_ref = refs
    else:
        w_ref, tile_ref, act_ref = refs
    half = D_MODEL // 2
    sub = lax.broadcasted_iota(jnp.int32, (8, half), 0)
    even = (lax.broadcasted_iota(jnp.int32, (1, 2 * NSEL), 1) % 2) == 0
    groups_per_head = ROUTE_TB // (PEER_HEADS * TOK_GROUP)

    def step(h, carry):
        if fuse_route:
            _route_head(q_ref, keys_ref, e_ref, g_ref, h)
        for gi in range(groups_per_head):
            t0 = (h * groups_per_head + gi) * TOK_GROUP
            _gather_group(idx_ref, tab_ref, tile_ref.at[gi], t0)
            for u in range(TOK_GROUP):
                hi, lo = _split_hi_lo(u_ref[pl.ds(t0 + u, 1), :])
                lhs = jnp.where(sub == 0, hi[:, :half], jnp.where(sub == 1, hi[:, half:],
                      jnp.where(sub == 2, lo[:, :half], jnp.where(sub == 3, lo[:, half:], 0.0))))
                r = lax.dot_general(lhs.astype(BF16), _packed_rows(tile_ref.at[gi], u), (NT, ((), ())),
                                    preferred_element_type=F32)
                act_ref[pl.ds(t0 + u, 1), :] = jnp.where(even, r[0:1] + r[2:3], r[1:2] + r[3:4])
        return carry

    lax.fori_loop(0, PEER_HEADS, step, 0)
    part = act_ref[...]
    lane_even = (lax.broadcasted_iota(jnp.int32, part.shape, 1) % 2) == 0
    act = part + jnp.where(lane_even, pltpu.roll(part, 2 * NSEL - 1, 1), pltpu.roll(part, 1, 1))
    w_ref[...] = gate_ref[...] * (0.5 * act * (1.0 + lax.erf(act * math.sqrt(0.5))))


def _peer_act_call(idx, u2, gate2, tab, row0, n_part, route=None):
    d = u2.shape[1]
    tb = ROUTE_TB
    b0 = row0 // tb
    groups_per_head = tb // (PEER_HEADS * TOK_GROUP)
    in_specs = [pl.BlockSpec((tb * NSEL,), lambda i: (i,), memory_space=pltpu.SMEM),
                pl.BlockSpec((tb, d), lambda i: (i + b0, 0)),
                pl.BlockSpec((tb, 2 * NSEL), lambda i: (i, 0)),
                pl.BlockSpec(memory_space=pltpu.VMEM)]
    out_specs = [pl.BlockSpec((tb, 2 * NSEL), lambda i: (i, 0))]
    out_shape = [jax.ShapeDtypeStruct((n_part, 2 * NSEL), F32)]
    args = [idx, u2, gate2, tab]
    if route is not None:
        qp, subkeys, q_row0 = route
        qb0 = q_row0 // tb
        in_specs += [pl.BlockSpec((tb, qp.shape[1]), lambda i: (i + qb0, 0)),
                     pl.BlockSpec(subkeys.shape, lambda i: (0, 0, 0))]
        rout = pl.BlockSpec((1, NSEL, tb), lambda i: (i, 0, 0))
        out_specs += [rout, rout]
        out_shape += [jax.ShapeDtypeStruct((n_part // tb, NSEL, tb), jnp.int32),
                      jax.ShapeDtypeStruct((n_part // tb, NSEL, tb), F32)]
        args += [qp, subkeys]
    return pl.pallas_call(
        functools.partial(_peer_act_kernel, fuse_route=route is not None),
        grid=(n_part // tb,),
        in_specs=in_specs,
        out_specs=out_specs,
        out_shape=out_shape,
        scratch_shapes=[pltpu.VMEM((groups_per_head, TOK_GROUP, PACK_ROWS * NSEL, LANES), jnp.int32),
                        pltpu.VMEM((tb, 2 * NSEL), F32)],
        compiler_params=_cparams(1),
        name="peer_act_route" if route is not None else "peer_act",
    )(*args)


def _peer_out_kernel(idx_ref, w_ref, x1_ref, mod_ref, l2w_ref, l2b_ref, tab_ref, x2_ref, tile_ref, acc_ref,
                     *, alpha):
    tb = w_ref.shape[0]
    half = D_MODEL // 2
    sub = lax.broadcasted_iota(jnp.int32, (8, 2 * NSEL), 0)
    even = (lax.broadcasted_iota(jnp.int32, (8, 2 * NSEL), 1) % 2) == 0

    def group(gi, carry):
        t0 = gi * TOK_GROUP
        _gather_group(idx_ref, tab_ref, tile_ref, t0)
        for u in range(TOK_GROUP):
            hi, lo = _split_hi_lo(w_ref[pl.ds(t0 + u, 1), :])
            lhs = jnp.where((sub == 0) & even, hi, jnp.where((sub == 1) & ~even, hi,
                  jnp.where((sub == 2) & even, lo, jnp.where((sub == 3) & ~even, lo, 0.0))))
            r = jnp.dot(lhs.astype(BF16), _packed_rows(tile_ref, u), preferred_element_type=F32)
            acc_ref[pl.ds(t0 + u, 1), 0:half] = r[0:1] + r[2:3]
            acc_ref[pl.ds(t0 + u, 1), half:D_MODEL] = r[1:2] + r[3:4]
        return carry

    lax.fori_loop(0, tb // TOK_GROUP, group, 0)
    m = mod_ref[0, 0]
    x2_ref[...] = _ln(alpha * x1_ref[...] + m[5:6] * acc_ref[...]) * l2w_ref[...] + l2b_ref[...]


def _peer_out_call(idx, w2, x1, mod, ln2_w, ln2_b, tab, alpha, tb, t_len, ctx_len):
    n, d = x1.shape
    seg = lambda i: (((i * tb) % t_len) >= ctx_len).astype(jnp.int32)
    return pl.pallas_call(
        functools.partial(_peer_out_kernel, alpha=alpha),
        grid=(n // tb,),
        in_specs=[pl.BlockSpec((tb * NSEL,), lambda i: (i,), memory_space=pltpu.SMEM),
                  pl.BlockSpec((tb, 2 * NSEL), lambda i: (i, 0)),
                  pl.BlockSpec((tb, d), lambda i: (i, 0)),
                  pl.BlockSpec((1, 1, 6, d), lambda i: ((i * tb) // t_len, seg(i), 0, 0)),
                  pl.BlockSpec((1, d), lambda i: (0, 0)),
                  pl.BlockSpec((1, d), lambda i: (0, 0)),
                  pl.BlockSpec(memory_space=pltpu.VMEM)],
        out_specs=pl.BlockSpec((tb, d), lambda i: (i, 0)),
        out_shape=jax.ShapeDtypeStruct((n, d), F32),
        scratch_shapes=[pltpu.VMEM((TOK_GROUP, PACK_ROWS * NSEL, LANES), jnp.int32),
                        pltpu.VMEM((tb, d), F32)],
        compiler_params=_cparams(1),
        name="peer_out",
    )(idx, w2, x1, mod, ln2_w.reshape(1, d), ln2_b.reshape(1, d), tab)


def _num_parts(n_tok):
    return next(p for p in range(PEER_PARTS, 0, -1) if n_tok % (p * ROUTE_TB) == 0)


def _peer_ffn(qp, u2, x1, mod, subkeys, peer_u, peer_v, ln2_w, ln2_b, alpha, t_len, ctx_len):
    n_tok, d = u2.shape
    parts = _num_parts(n_tok)
    n_part = n_tok // parts
    tab_u = _pack_table(peer_u)
    e_t, gate_t = _peer_route_call(qp, subkeys, ROUTE_TB, n_part)
    idx_parts, w2_parts = [], []
    for p in range(parts):
        idx_p = (jnp.swapaxes(e_t, 1, 2).reshape(n_part * NSEL) * PACK_ROWS).astype(jnp.int32)
        gate2_p = jnp.repeat(jnp.swapaxes(gate_t, 1, 2).reshape(n_part, NSEL), 2, axis=1)
        idx_parts.append(idx_p)
        if p + 1 < parts:
            w2_p, e_t, gate_t = _peer_act_call(idx_p, u2, gate2_p, tab_u, p * n_part, n_part,
                                               route=(qp, subkeys, (p + 1) * n_part))
        else:
            w2_p, = _peer_act_call(idx_p, u2, gate2_p, tab_u, p * n_part, n_part)
        w2_parts.append(w2_p)
    idx = jnp.concatenate(idx_parts)
    w2 = jnp.concatenate(w2_parts, axis=0)
    return _peer_out_call(idx, w2, x1, mod, ln2_w, ln2_b, _pack_table(peer_v), alpha, 64, t_len, ctx_len)


def _rope_tables(n_rows, ctx_len):
    row = jnp.repeat(jnp.arange(n_rows), GRID_W).astype(F32)
    col = jnp.tile(jnp.arange(GRID_W), n_rows).astype(F32)
    quarter = HEAD_DIM // 4
    inv_freq = ROPE_BASE ** (-2.0 * jnp.arange(quarter, dtype=F32) / (HEAD_DIM // 2))
    ang_r = row[:, None] * inv_freq
    ang_c = col[:, None] * inv_freq
    cr, sr, cc, sc = jnp.cos(ang_r), jnp.sin(ang_r), jnp.cos(ang_c), jnp.sin(ang_c)
    cos = jnp.concatenate([cr, cr, cc, cc], axis=-1)
    sin = jnp.concatenate([-sr, sr, -sc, sc], axis=-1)
    cos = jnp.concatenate([jnp.ones((ctx_len, HEAD_DIM), F32), cos], axis=0)
    sin = jnp.concatenate([jnp.zeros((ctx_len, HEAD_DIM), F32), sin], axis=0)
    reps = QK_COLS // HEAD_DIM
    return jnp.tile(cos, (1, reps)), jnp.tile(sin, (1, reps))


def _swap_cols():
    q = HEAD_DIM // 4
    one = np.concatenate([np.arange(q, 2 * q), np.arange(0, q), np.arange(3 * q, 4 * q), np.arange(2 * q, 3 * q)])
    return np.concatenate([one + HEAD_DIM * i for i in range(QK_COLS // HEAD_DIM)])


def kernel(x, c, ctx, c_ctx, ada_w, ada_b, w_in, rwkv_conv, rwkv_w0, rwkv_w2, rwkv_a0, rwkv_a2, rwkv_g2, rwkv_k_k, rwkv_k_a, rwkv_r_k, rwkv_ln_w, rwkv_ln_b, diff_lambda, diff_subln_w, hgrn_lb_logits, hgrn_norm_w, w_out, ln1_w, ln1_b, peer_wq, peer_subkeys, peer_u, peer_v, ln2_w, ln2_b):
    bsz, seq, d = x.shape
    ctx_len = ctx.shape[1]
    depth = w_in.shape[0]
    t_len = ctx_len + seq
    tm = 256 if ctx_len % 256 == 0 else 128
    assert ctx_len % tm == 0 and seq % tm == 0 and seq % GRID_W == 0
    ctx_blocks = ctx_len // tm
    alpha = (2.0 * depth) ** 0.25

    consts = _scan_consts()
    hsum = jnp.asarray((np.arange(A_WIDTH)[:, None] // HEAD_DIM == np.arange(A_WIDTH)[None, :] // HEAD_DIM)
                       .astype(np.float32))
    cos, sin = _rope_tables(seq // GRID_W, ctx_len)
    swap = _swap_cols()

    lb_p = jax.nn.softmax(hgrn_lb_logits.astype(F32), axis=0)
    lower_bounds = jnp.cumsum(lb_p, axis=0) - lb_p[0]

    n_cond = 8 * ((bsz + 1 + 7) // 8)
    cc = jnp.zeros((n_cond, d), F32).at[0].set(c_ctx).at[1:1 + bsz].set(c)
    mods = _ada_call(cc, ada_w, ada_b)

    xs = jnp.concatenate([ctx, x], axis=1)
    for l in range(depth):
        last = l == depth - 1
        m = mods[l].reshape(n_cond, 6, d)
        mod = jnp.stack([jnp.broadcast_to(m[0], (bsz, 6, d)), m[1:1 + bsz]], axis=1)
        q_cols = w_in[l][:, A_COLS:A_COLS + QK_COLS]
        k_cols = w_in[l][:, A_COLS + QK_COLS:A_COLS + 2 * QK_COLS]
        w_ext = jnp.concatenate([w_in[l], q_cols[:, swap], k_cols[:, swap]], axis=1).astype(BF16)
        P = dict(rwkv_conv=rwkv_conv[l], rwkv_w0=rwkv_w0[l], rwkv_w2=rwkv_w2[l], rwkv_a0=rwkv_a0[l],
                 rwkv_a2=rwkv_a2[l], rwkv_g2=rwkv_g2[l], rwkv_k_k=rwkv_k_k[l], rwkv_k_a=rwkv_k_a[l],
                 rwkv_r_k=rwkv_r_k[l], rwkv_ln_w=rwkv_ln_w[l], rwkv_ln_b=rwkv_ln_b[l],
                 hgrn_norm_w=hgrn_norm_w[l], w_out_bf16=w_out[l].astype(BF16),
                 ln1_w=ln1_w[l], ln1_b=ln1_b[l], peer_wq_bf16=peer_wq[l].astype(BF16))

        oa, q, k, v, oc = _inproj_call(xs, mod, w_ext, cos, sin, tm, ctx_blocks)
        r, k2, vv, avec, bvec, lw, g, bonus = _rwkv_prep_call(oa, P, hsum, tm, ctx_blocks)
        rw_o = _rwkv_scan_call(r, k2, vv, avec, bvec, lw, consts, ctx_len // CHUNK)
        hg_o = _hgrn_scan_call(oc, lower_bounds[l], consts, ctx_len // CHUNK)
        lam_init = 0.8 - 0.6 * math.exp(-0.3 * l)
        row0 = ctx_blocks if last else 0
        t_out = t_len - row0 * tm
        yb = _attn_call(q, k, v, diff_lambda[l], diff_subln_w[l], lam_init, tm, ctx_len, skip_ctx=last)
        x1, u2, qp = _outproj_call(xs, mod, rw_o, g, bonus, yb, hg_o, oc, hsum, P, alpha, tm, ctx_blocks, row0)

        x2 = _peer_ffn(qp.reshape(bsz * t_out, -1), u2.reshape(bsz * t_out, d), x1.reshape(bsz * t_out, d), mod,
                       peer_subkeys[l], peer_u[l], peer_v[l], ln2_w[l], ln2_b[l], alpha, t_out,
                       ctx_len - row0 * tm)
        xs = x2.reshape(bsz, t_out, d)
    return xs
```

```python
import functools
import math

import numpy as np
import jax
import jax.numpy as jnp
from jax import lax
from jax.experimental import pallas as pl
from jax.experimental.pallas import tpu as pltpu

F32 = jnp.float32
BF16 = jnp.bfloat16
HIGHEST = lax.Precision.HIGHEST

D_MODEL = 1024
GRID_W = 64
HEAD_DIM = 64
A_WIDTH = 256
A_DECAY_LORA = 32
A_ICLR_LORA = 32
A_GATE_LORA = 64
B_WIDTH = 512
B_HEADS = 4
B_V_DIM = 128
C_WIDTH = 256
ROPE_BASE = 10000.0
MIN_FORGET = 1e-30
PEER_HEADS = 8
PEER_NKEYS = 128
PEER_TOPK = 16
PEER_QDIM = 256
LN_EPS = 1e-5
RWKV_GN_EPS = 64e-5

A_COLS = 3 * A_WIDTH + A_DECAY_LORA + A_ICLR_LORA + A_GATE_LORA
QK_COLS = 2 * B_HEADS * HEAD_DIM
C_COLS = 5 * C_WIDTH
IN_WIDTH = A_COLS + 3 * QK_COLS + C_COLS
SCAN_HEADS = 4
CHUNK = 64
STACK = SCAN_HEADS * CHUNK
SCAN_BATCH = 2
LANES = 128
VMEM_LIMIT = 56 * 1024 * 1024


def _cparams(n_axes, vmem=None):
    return pltpu.CompilerParams(dimension_semantics=("arbitrary",) * n_axes,
                                vmem_limit_bytes=vmem or VMEM_LIMIT)


def _ln(x):
    xc = x - jnp.mean(x, axis=-1, keepdims=True)
    return xc * lax.rsqrt(jnp.mean(xc * xc, axis=-1, keepdims=True) + LN_EPS)


def _mm(a, b, dims=((1,), (0,)), exact=False):
    dn = (dims, ((), ()))
    if exact is True:
        return lax.dot_general(a, b, dn, precision=HIGHEST, preferred_element_type=F32)
    dot = lambda p, q: lax.dot_general(p, q, dn, preferred_element_type=F32)
    a_hi = a.astype(BF16)
    b_hi = b.astype(BF16)
    if exact == "rhs3":
        r1 = b - b_hi.astype(F32)
        b_mid = r1.astype(BF16)
        b_lo = (r1 - b_mid.astype(F32)).astype(BF16)
        return dot(a_hi, b_hi) + (dot(a_hi, b_mid) + dot(a_hi, b_lo))
    if exact == "lhs3":
        r1 = a - a_hi.astype(F32)
        a_mid = r1.astype(BF16)
        a_lo = (r1 - a_mid.astype(F32)).astype(BF16)
        return dot(a_hi, b_hi) + (dot(a_mid, b_hi) + dot(a_lo, b_hi))
    if exact == "x3":
        a_lo = (a - a_hi.astype(F32)).astype(BF16)
        b_lo = (b - b_hi.astype(F32)).astype(BF16)
        return dot(a_hi, b_hi) + (dot(a_hi, b_lo) + dot(a_lo, b_hi))
    return dot(a_hi, b_hi)


NT = ((1,), (1,))
TN = ((0,), (0,))


def _ada_kernel(c_ref, w_ref, b_ref, o_ref):
    c = c_ref[...]
    s = c * jax.nn.sigmoid(c)
    o_ref[...] = _mm(s, w_ref[...], exact=True) + b_ref[...]


def _ada_call(cc, ada_w, ada_b):
    depth, d, n = ada_w.shape
    bn = 512
    return pl.pallas_call(
        _ada_kernel,
        grid=(depth, n // bn),
        in_specs=[pl.BlockSpec(cc.shape, lambda l, j: (0, 0)),
                  pl.BlockSpec((None, d, bn), lambda l, j: (l, 0, j)),
                  pl.BlockSpec((None, 1, bn), lambda l, j: (l, 0, j))],
        out_specs=pl.BlockSpec((None, cc.shape[0], bn), lambda l, j: (l, 0, j)),
        out_shape=jax.ShapeDtypeStruct((depth, cc.shape[0], n), F32),
        compiler_params=_cparams(2),
        name="ada_mod",
    )(cc, ada_w, ada_b.reshape(depth, 1, n))


def _inproj_kernel(x_ref, mod_ref, w_ref, cos_ref, sin_ref, oa_ref, q_ref, k_ref, v_ref, oc_ref):
    m = mod_ref[0, 0]
    u = _ln(x_ref[0]) * (1.0 + m[1:2]) + m[0:1]
    h = _mm(u, w_ref[...])
    oa_ref[0] = h[:, 0:A_COLS]
    cos = cos_ref[...]
    sin = sin_ref[...]
    q0 = A_COLS
    k0 = q0 + QK_COLS
    v0 = k0 + QK_COLS
    c0 = v0 + QK_COLS
    qs0 = IN_WIDTH
    ks0 = qs0 + QK_COLS
    q = (h[:, q0:k0] * cos + h[:, qs0:ks0] * sin) * (math.log2(math.e) * HEAD_DIM ** -0.5)
    k = h[:, k0:v0] * cos + h[:, ks0:ks0 + QK_COLS] * sin
    q_ref[0] = q.astype(BF16)
    k_ref[0] = k.astype(BF16)
    v_ref[0] = h[:, v0:c0].astype(BF16)
    oc_ref[0] = h[:, c0:IN_WIDTH]


def _inproj_call(x, mod, w_ext, cos, sin, tm, ctx_blocks):
    b, t, d = x.shape
    n_ext = w_ext.shape[1]
    tok = lambda width: pl.BlockSpec((1, tm, width), lambda i, j: (i, j, 0))
    return pl.pallas_call(
        _inproj_kernel,
        grid=(b, t // tm),
        in_specs=[tok(d),
                  pl.BlockSpec((1, 1, 6, d), lambda i, j: (i, (j >= ctx_blocks).astype(jnp.int32), 0, 0)),
                  pl.BlockSpec((d, n_ext), lambda i, j: (0, 0)),
                  pl.BlockSpec((tm, QK_COLS), lambda i, j: (j, 0)),
                  pl.BlockSpec((tm, QK_COLS), lambda i, j: (j, 0))],
        out_specs=[tok(A_COLS), tok(QK_COLS), tok(QK_COLS), tok(QK_COLS), tok(C_COLS)],
        out_shape=[jax.ShapeDtypeStruct((b, t, A_COLS), F32),
                   jax.ShapeDtypeStruct((b, t, QK_COLS), BF16),
                   jax.ShapeDtypeStruct((b, t, QK_COLS), BF16),
                   jax.ShapeDtypeStruct((b, t, QK_COLS), BF16),
                   jax.ShapeDtypeStruct((b, t, C_COLS), F32)],
        compiler_params=_cparams(2),
        name="in_proj",
    )(x, mod, w_ext, cos, sin)


def _rwkv_prep_kernel(cur_ref, prev_ref, next_ref, conv_ref, w0_ref, w2_ref, a0_ref, a2_ref, g2_ref,
                      kk_ref, ka_ref, rk_ref, hsum_ref,
                      r_o, k_o, v_o, a_o, b_o, lw_o, g_o, bonus_o, *, ctx_blocks, n_blocks):
    j = pl.program_id(1)
    cur = cur_ref[0]
    tm = cur.shape[0]
    w3 = 3 * A_WIDTH
    rkv = cur[:, 0:w3]
    has_prev = jnp.logical_and(j != 0, j != ctx_blocks)
    has_next = jnp.logical_and(j != ctx_blocks - 1, j != n_blocks - 1)
    prev_row = jnp.where(has_prev, prev_ref[0][7:8, 0:w3], 0.0)
    next_row = jnp.where(has_next, next_ref[0][0:1, 0:w3], 0.0)
    row = lax.broadcasted_iota(jnp.int32, (tm, w3), 0)
    xm1 = jnp.where(row == 0, prev_row, pltpu.roll(rkv, 1, 0))
    xp1 = jnp.where(row == tm - 1, next_row, pltpu.roll(rkv, tm - 1, 0))
    cw = conv_ref[...]
    conv = cw[0:1] * xm1 + cw[1:2] * rkv + cw[2:3] * xp1
    r = conv[:, 0:A_WIDTH]
    k = conv[:, A_WIDTH:2 * A_WIDTH]
    v = conv[:, 2 * A_WIDTH:w3]
    o = w3
    w_lo = cur[:, o:o + A_DECAY_LORA]
    a_lo = cur[:, o + A_DECAY_LORA:o + A_DECAY_LORA + A_ICLR_LORA]
    g_lo = cur[:, o + A_DECAY_LORA + A_ICLR_LORA:A_COLS]
    a = jax.nn.sigmoid(a0_ref[...] + _mm(a_lo, a2_ref[...], exact=True))
    g = _mm(jax.nn.sigmoid(g_lo), g2_ref[...], exact=True)
    hsum = hsum_ref[...]
    kk = k * kk_ref[...]
    ss = _mm(kk * kk, hsum, exact="lhs3")
    kk = kk * lax.rsqrt(jnp.maximum(ss, 1e-24))
    k2 = k * (1.0 + (a - 1.0) * ka_ref[...])
    tw = jnp.tanh(w_lo)
    w0 = w0_ref[...]
    for d in range(2):
        wl = w0[d:d + 1] + _mm(tw, w2_ref[d], exact=True)
        lw_o[d, 0] = -math.exp(-0.5) * jax.nn.sigmoid(wl)
    r_o[0] = r
    k_o[0] = k2
    v_o[0] = v
    a_o[0] = -kk
    b_o[0] = kk * a
    g_o[0] = g
    bonus_o[0] = _mm(r * k2 * rk_ref[...], hsum, exact="lhs3") * v


def _rwkv_prep_call(oa, P, hsum, tm, ctx_blocks):
    b, t, _ = oa.shape
    n_blocks = t // tm
    per8 = tm // 8
    last8 = t // 8 - 1
    full = lambda arr: pl.BlockSpec(arr.shape, lambda i, j: (0,) * arr.ndim)
    tok = pl.BlockSpec((1, tm, A_WIDTH), lambda i, j: (i, j, 0))
    row = lambda p: p.reshape(1, -1)
    params = [P['rwkv_conv'], P['rwkv_w0'], P['rwkv_w2'], row(P['rwkv_a0']), P['rwkv_a2'], P['rwkv_g2'],
              row(P['rwkv_k_k']), row(P['rwkv_k_a']), row(P['rwkv_r_k']), hsum]
    shp = jax.ShapeDtypeStruct((b, t, A_WIDTH), F32)
    return pl.pallas_call(
        functools.partial(_rwkv_prep_kernel, ctx_blocks=ctx_blocks, n_blocks=n_blocks),
        grid=(b, n_blocks),
        in_specs=[pl.BlockSpec((1, tm, A_COLS), lambda i, j: (i, j, 0)),
                  pl.BlockSpec((1, 8, A_COLS), lambda i, j: (i, jnp.maximum(j * per8 - 1, 0), 0)),
                  pl.BlockSpec((1, 8, A_COLS), lambda i, j: (i, jnp.minimum((j + 1) * per8, last8), 0))]
                 + [full(p) for p in params],
        out_specs=[tok, tok, tok, tok, tok,
                   pl.BlockSpec((2, 1, tm, A_WIDTH), lambda i, j: (0, i, j, 0)), tok, tok],
        out_shape=[shp, shp, shp, shp, shp, jax.ShapeDtypeStruct((2, b, t, A_WIDTH), F32), shp, shp],
        compiler_params=_cparams(2),
        name="rwkv_prep",
    )(oa, oa, oa, *params)


def _scan_consts():
    c = CHUNK
    t = np.arange(c)
    tri = np.zeros((2, c, c), np.float32)
    tri[0] = (t[None, :] <= t[:, None])
    tri[1] = (t[None, :] >= t[:, None])
    st = np.arange(STACK)
    same_head = (st[:, None] // c) == (st[None, :] // c)
    tt = st[:, None] % c
    ss = st[None, :] % c
    strict = np.stack([same_head & (ss < tt), same_head & (ss > tt)]).astype(np.float32)
    incl = np.stack([same_head & (ss <= tt), same_head & (ss >= tt)]).astype(np.float32)
    head_mask = ((st[:, None] // c) == (np.arange(SCAN_HEADS * HEAD_DIM)[None, :] // HEAD_DIM)).astype(np.float32)
    n_lv = int(math.log2(c))
    seg = np.zeros((2, (n_lv + 1) * c, c), np.float32)
    seg[0, :c] = tri[0]
    seg[1, :c] = tri[1]
    level = -np.ones((2, STACK, STACK), np.int32)
    for d in range(2):
        level[d][same_head & (tt == ss)] = 0
    for l in range(1, n_lv + 1):
        n = 2 ** l
        for ti in range(c):
            s0 = (ti // n) * n
            m = s0 + n // 2 - 1
            hh = s0 + n // 2
            if ti > m:
                seg[0, l * c + ti, m + 1:ti + 1] = 1.0
            else:
                seg[0, l * c + ti, ti + 1:m + 1] = 1.0
            if ti < hh:
                seg[1, l * c + ti, ti:hh] = 1.0
            else:
                seg[1, l * c + ti, hh:ti] = 1.0
        same_blk = (tt // n) == (ss // n)
        t_second = (tt % n) >= n // 2
        s_second = (ss % n) >= n // 2
        level[0][same_head & same_blk & t_second & ~s_second] = l
        level[1][same_head & same_blk & ~t_second & s_second] = l
    return dict(tri=jnp.asarray(tri), strict=jnp.asarray(strict), incl=jnp.asarray(incl),
                head_mask=jnp.asarray(head_mask), seg=jnp.asarray(seg), level=jnp.asarray(level))


def _tile4(x):
    return jnp.concatenate([x] * SCAN_HEADS, axis=0)


def _fold4(x):
    c = CHUNK
    return x[0:c] + x[c:2 * c] + x[2 * c:3 * c] + x[3 * c:4 * c]


def _chunk_index(d, j, nc_ctx, nc):
    if d == 0:
        return j
    return jnp.where(j < nc_ctx, nc_ctx - 1 - j, nc - 1 - (j - nc_ctx))


def _rwkv_chunk(tri, strict, incl, hm, r, k, v, a, b, lw, z, mm, mmi):
    cum = _mm(tri, lw, exact="rhs3")
    tot = jnp.sum(lw, axis=0, keepdims=True)
    e_neg = jnp.exp(-cum)
    e_end = jnp.exp(tot - cum)
    a_sm = _tile4(a * jnp.exp(cum - lw)) * hm
    r_sm = _tile4(r * jnp.exp(cum)) * hm
    v_sm = _tile4(v) * hm
    b_rep = _tile4(b * e_neg)
    k_rep = _tile4(k * e_neg)
    zero = jnp.zeros((STACK, STACK), F32)
    n_ab = jnp.where(strict, mm(a_sm, b_rep, NT), zero)
    n_ak = jnp.where(strict, mm(a_sm, k_rep, NT), zero)
    n_rb = jnp.where(incl, mm(r_sm, b_rep, NT), zero)
    n_rk = jnp.where(incl, mm(r_sm, k_rep, NT), zero)
    rows = lax.broadcasted_iota(jnp.int32, (STACK, STACK), 0)
    cols = lax.broadcasted_iota(jnp.int32, (STACK, STACK), 1)
    inv = jnp.where(rows == cols, 1.0, 0.0) + n_ab
    npow = n_ab
    for _ in range(int(math.log2(CHUNK)) - 1):
        npow = mmi(npow, npow)
        inv = inv + mmi(inv, npow)
    u_sm = mm(inv, mm(a_sm, z, NT) + mm(n_ak, v_sm))
    o_sm = mm(r_sm, z, NT) + mm(n_rb, u_sm) + mm(n_rk, v_sm)
    z_new = z * jnp.exp(tot) + mm(u_sm, _tile4(b * e_end) * hm, TN) + mm(v_sm, _tile4(k * e_end) * hm, TN)
    return _fold4(o_sm), z_new


def _rwkv_scan_kernel(tri_ref, strict_ref, incl_ref, hm_ref, *refs, exact, inv_exact):
    ins, (of_ref, ob_ref, z_ref) = refs[:12], refs[12:]

    @pl.when(pl.program_id(1) == 0)
    def _():
        z_ref[...] = jnp.zeros_like(z_ref)

    mm = functools.partial(_mm, exact=exact)
    mmi = functools.partial(_mm, exact=inv_exact)
    hm = hm_ref[...]
    for bb in range(SCAN_BATCH):
        for d, o_ref in enumerate((of_ref, ob_ref)):
            r, k, v, a, b = (ref[bb] for ref in ins[6 * d:6 * d + 5])
            o, z_new = _rwkv_chunk(tri_ref[d], strict_ref[d] > 0.0, incl_ref[d] > 0.0, hm, r, k, v, a, b,
                                   ins[6 * d + 5][0, bb], z_ref[2 * bb + d], mm, mmi)
            o_ref[bb] = o
            z_ref[2 * bb + d] = z_new


def _rwkv_scan_call(r, k, v, a, bvec, lw, consts, nc_ctx, exact=False, inv_exact=False):
    b, t, w = r.shape
    nc = t // CHUNK
    full = lambda arr: pl.BlockSpec(arr.shape, lambda i, j: (0,) * arr.ndim)
    specs, args = [], []
    for d in range(2):
        cidx = functools.partial(_chunk_index, d, nc_ctx=nc_ctx, nc=nc)
        tok = pl.BlockSpec((SCAN_BATCH, CHUNK, w), lambda i, j, cidx=cidx: (i, cidx(j), 0))
        specs += [tok] * 5 + [pl.BlockSpec((1, SCAN_BATCH, CHUNK, w),
                                           lambda i, j, cidx=cidx, d=d: (d, i, cidx(j), 0))]
        args += [r, k, v, a, bvec, lw]
    out_specs = [pl.BlockSpec((SCAN_BATCH, CHUNK, w), lambda i, j, d=d: (i, _chunk_index(d, j, nc_ctx, nc), 0))
                 for d in range(2)]
    cs = [consts['tri'], consts['strict'], consts['incl'], consts['head_mask']]
    return pl.pallas_call(
        functools.partial(_rwkv_scan_kernel, exact=exact, inv_exact=inv_exact),
        grid=(b // SCAN_BATCH, nc),
        in_specs=[full(c) for c in cs] + specs,
        out_specs=out_specs,
        out_shape=[jax.ShapeDtypeStruct((b, t, w), F32)] * 2,
        scratch_shapes=[pltpu.VMEM((2 * SCAN_BATCH, STACK, STACK), F32)],
        compiler_params=_cparams(2),
        name="rwkv_scan",
    )(*cs, *args)


def _hgrn_chunk(seg, level, hm, lb, qr, zf, v, z, mm):
    q = qr * jax.nn.sigmoid(qr)
    f = lb + (1.0 - lb) * jax.nn.sigmoid(zf)
    logf = jnp.log(jnp.maximum(f, MIN_FORGET))
    k = (1.0 - lb) * jax.nn.sigmoid(-zf)
    segs = _mm(seg, logf, exact="rhs3")
    cum = segs[0:CHUNK]
    tot = jnp.sum(logf, axis=0, keepdims=True)
    attn = jnp.where(level == 0, mm(_tile4(q) * hm, _tile4(k), NT), 0.0)
    for l in range(1, int(math.log2(CHUNK)) + 1):
        e = jnp.exp(segs[l * CHUNK:(l + 1) * CHUNK])
        attn = jnp.where(level == l, mm(_tile4(q * e) * hm, _tile4(k * e), NT), attn)
    v_sm = _tile4(v) * hm
    o_sm = mm(_tile4(q * jnp.exp(cum)) * hm, z, NT) + mm(attn, v_sm)
    z_new = z * jnp.exp(tot) + mm(v_sm, _tile4(k * jnp.exp(tot - cum)) * hm, TN)
    return _fold4(o_sm), z_new


def _hgrn_scan_kernel(seg_ref, level_ref, hm_ref, lb_ref, *refs, exact):
    ins, (of_ref, ob_ref, z_ref) = refs[:6], refs[6:]

    @pl.when(pl.program_id(1) == 0)
    def _():
        z_ref[...] = jnp.zeros_like(z_ref)

    mm = functools.partial(_mm, exact=exact)
    hm = hm_ref[...]
    lb = lb_ref[...]
    for bb in range(SCAN_BATCH):
        for d, o_ref in enumerate((of_ref, ob_ref)):
            qr, zf, v = (ref[bb] for ref in ins[3 * d:3 * d + 3])
            o, z_new = _hgrn_chunk(seg_ref[d], level_ref[d], hm, lb, qr, zf, v, z_ref[2 * bb + d], mm)
            o_ref[bb] = o
            z_ref[2 * bb + d] = z_new


def _hgrn_scan_call(oc, lb, consts, nc_ctx, exact=False):
    b, t, _ = oc.shape
    w = C_WIDTH
    nc = t // CHUNK
    full = lambda arr: pl.BlockSpec(arr.shape, lambda i, j: (0,) * arr.ndim)
    specs = []
    for d in range(2):
        cidx = functools.partial(_chunk_index, d, nc_ctx=nc_ctx, nc=nc)
        specs += [pl.BlockSpec((SCAN_BATCH, CHUNK, w), lambda i, j, cidx=cidx, col=col: (i, cidx(j), col))
                  for col in (0, 1 + d, 3)]
    out_specs = [pl.BlockSpec((SCAN_BATCH, CHUNK, w), lambda i, j, d=d: (i, _chunk_index(d, j, nc_ctx, nc), 0))
                 for d in range(2)]
    cs = [consts['seg'], consts['level'], consts['head_mask'], lb.reshape(1, w)]
    return pl.pallas_call(
        functools.partial(_hgrn_scan_kernel, exact=exact),
        grid=(b // SCAN_BATCH, nc),
        in_specs=[full(c) for c in cs] + specs,
        out_specs=out_specs,
        out_shape=[jax.ShapeDtypeStruct((b, t, w), F32)] * 2,
        scratch_shapes=[pltpu.VMEM((2 * SCAN_BATCH, STACK, STACK), F32)],
        compiler_params=_cparams(2),
        name="hgrn_scan",
    )(*cs, *([oc] * 6))


def _attn_kernel(lam_ref, w_ref, q_ref, k_ref, v_ref, o_ref, *, lam_init, ctx_len, ctx_blocks):
    j = pl.program_id(2)
    ll = lam_ref[...]
    lam = (jnp.exp(jnp.sum(ll[0:1] * ll[1:2], axis=-1, keepdims=True))
           - jnp.exp(jnp.sum(ll[2:3] * ll[3:4], axis=-1, keepdims=True)) + lam_init)
    q = q_ref[0]
    first = lax.broadcasted_iota(jnp.int32, q.shape, 1) < HEAD_DIM
    zq = jnp.zeros_like(q)
    q1 = jnp.where(first, q, zq)
    q2 = jnp.where(first, zq, q)

    def attend(k, v):
        def softmax_parts(qm):
            s = lax.dot_general(qm, k, (NT, ((), ())), preferred_element_type=F32)
            p = jnp.exp2(s - jnp.max(s, axis=-1, keepdims=True))
            return p, 1.0 / jnp.sum(p, axis=-1, keepdims=True)
        p1, i1 = softmax_parts(q1)
        p2, i2 = softmax_parts(q2)
        a = p1 * i1 - p2 * (lam * i2)
        o = jnp.dot(a.astype(BF16), v, preferred_element_type=F32)
        y = o * lax.rsqrt(jnp.mean(o * o, axis=-1, keepdims=True) + LN_EPS)
        o_ref[0] = y * w_ref[...] * (1.0 - lam_init)

    @pl.when(j < ctx_blocks)
    def _():
        attend(k_ref[0, 0:ctx_len], v_ref[0, 0:ctx_len])

    @pl.when(j >= ctx_blocks)
    def _():
        attend(k_ref[0], v_ref[0])


def _attn_call(q, k, v, diff_lambda, subln_w, lam_init, tq, ctx_len, skip_ctx):
    b, t, _ = q.shape
    ctx_blocks = ctx_len // tq
    j0 = ctx_blocks if skip_ctx else 0
    kv = pl.BlockSpec((1, t, B_V_DIM), lambda i, h, j: (i, 0, h))
    qo = pl.BlockSpec((1, tq, B_V_DIM), lambda i, h, j: (i, j + j0, h))
    return pl.pallas_call(
        functools.partial(_attn_kernel, lam_init=lam_init, ctx_len=ctx_len, ctx_blocks=ctx_blocks - j0),
        grid=(b, B_HEADS, t // tq - j0),
        in_specs=[pl.BlockSpec((4, HEAD_DIM), lambda i, h, j: (0, 0)),
                  pl.BlockSpec((1, B_V_DIM), lambda i, h, j: (0, 0)),
                  qo, kv, kv],
        out_specs=qo,
        out_shape=jax.ShapeDtypeStruct((b, t, B_WIDTH), F32),
        compiler_params=_cparams(3),
        name="diff_attn",
    )(diff_lambda, subln_w.reshape(1, B_V_DIM), q, k, v)


def _outproj_kernel(x_ref, mod_ref, of_ref, ob_ref, g_ref, bonus_ref, yb_ref, cf_ref, cb_ref, cg_ref,
                    hsum_ref, lnw_ref, lnb_ref, hw_ref, wout_ref, l1w_ref, l1b_ref, wq_ref,
                    x1_ref, u2_ref, qp_ref, *, alpha):
    m = mod_ref[0, 0]
    hsum = hsum_ref[...]
    inv = 1.0 / HEAD_DIM
    o = of_ref[0] + ob_ref[0]
    oc = o - _mm(o, hsum, exact="lhs3") * inv
    ya = oc * lax.rsqrt(_mm(oc * oc, hsum, exact="lhs3") * inv + RWKV_GN_EPS)
    ya = (ya * lnw_ref[...] + lnb_ref[...] + bonus_ref[0]) * g_ref[0]
    c = cf_ref[0] + cb_ref[0]
    cg = cg_ref[0]
    yc = c * lax.rsqrt(_mm(c * c, hsum, exact="lhs3") * inv + LN_EPS) * hw_ref[...] * (cg * jax.nn.sigmoid(cg))
    y = jnp.concatenate([ya, yb_ref[0], yc], axis=-1)
    proj = _mm(y, wout_ref[...])
    x1 = _ln(alpha * x_ref[0] + m[2:3] * proj) * l1w_ref[...] + l1b_ref[...]
    x1_ref[0] = x1
    u2 = _ln(x1) * (1.0 + m[4:5]) + m[3:4]
    u2_ref[0] = u2
    qp_ref[0] = _mm(u2, wq_ref[...])


def _outproj_call(x, mod, rw_o, rw_g, rw_bonus, yb, hg_o, oc, hsum, P, alpha, tm, ctx_blocks, row0):
    b, t, d = x.shape
    nq = P['peer_wq_bf16'].shape[1]
    t_out = t - row0 * tm
    tok = lambda width, col=0: pl.BlockSpec((1, tm, width), lambda i, j: (i, j + row0, col))
    otok = lambda width: pl.BlockSpec((1, tm, width), lambda i, j: (i, j, 0))
    full = lambda arr: pl.BlockSpec(arr.shape, lambda i, j: (0,) * arr.ndim)
    row = lambda p: p.reshape(1, -1)
    params = [hsum, row(P['rwkv_ln_w']), row(P['rwkv_ln_b']), row(jnp.tile(P['hgrn_norm_w'], SCAN_HEADS)),
              P['w_out_bf16'], row(P['ln1_w']), row(P['ln1_b']), P['peer_wq_bf16']]
    return pl.pallas_call(
        functools.partial(_outproj_kernel, alpha=alpha),
        grid=(b, t_out // tm),
        in_specs=[tok(d),
                  pl.BlockSpec((1, 1, 6, d), lambda i, j: (i, (j + row0 >= ctx_blocks).astype(jnp.int32), 0, 0)),
                  tok(A_WIDTH), tok(A_WIDTH), tok(A_WIDTH), tok(A_WIDTH), tok(B_WIDTH),
                  tok(C_WIDTH), tok(C_WIDTH), tok(C_WIDTH, 4)] + [full(p) for p in params],
        out_specs=[otok(d), otok(d), otok(nq)],
        out_shape=[jax.ShapeDtypeStruct((b, t_out, d), F32), jax.ShapeDtypeStruct((b, t_out, d), F32),
                   jax.ShapeDtypeStruct((b, t_out, nq), F32)],
        compiler_params=_cparams(2),
        name="out_proj",
    )(x, mod, rw_o[0], rw_o[1], rw_g, rw_bonus, yb, hg_o[0], hg_o[1], oc, *params)


def _topk_rows(s, order=None, payload=None):
    if order is None:
        order = lax.broadcasted_iota(jnp.int32, s.shape, 0)
    big = jnp.int32(2 ** 30)
    vals, picks = [], []
    for _ in range(PEER_TOPK):
        m = jnp.max(s, axis=0, keepdims=True)
        idx = jnp.min(jnp.where(s == m, order, big), axis=0, keepdims=True)
        hit = order == idx
        vals.append(m)
        picks.append(idx if payload is None else jnp.max(jnp.where(hit, payload, -1), axis=0, keepdims=True))
        s = jnp.where(hit, -jnp.inf, s)
    return jnp.concatenate(vals, axis=0), jnp.concatenate(picks, axis=0)


def _pair_candidates(sv, si):
    k = PEER_TOPK
    vals, flat, eid = [], [], []
    tokens = sv[0].shape[1]
    row8 = lax.broadcasted_iota(jnp.int32, (8, tokens), 0)
    row16 = lax.broadcasted_iota(jnp.int32, (k, tokens), 0)
    neg = -jnp.inf
    for i in range(8):
        n_valid = k // (i + 1)
        rows, row = (k, row16) if n_valid > 8 else (8, row8)
        v = sv[0][i:i + 1] + sv[1][0:rows]
        vals.append(v if n_valid == rows else jnp.where(row < n_valid, v, neg))
        flat.append(i * k + row)
        eid.append(si[0][i:i + 1] * PEER_NKEYS + si[1][0:rows])
    vals.append(sv[0][8:k] + sv[1][0:1])
    flat.append((row8 + 8) * k)
    eid.append(si[0][8:k] * PEER_NKEYS + si[1][0:1])
    cat = lambda xs: jnp.concatenate(xs, axis=0)
    return cat(vals), cat(flat), cat(eid)


def _route_head(q_ref, keys_ref, e_ref, g_ref, h):
    half = PEER_QDIM // 2
    sv, si = [], []
    for p in range(2):
        start = pl.multiple_of((2 * h + p) * half, half)
        qhp = q_ref[:, pl.ds(start, half)]
        s = _mm(keys_ref[p], qhp, NT, exact=True)
        vals, idx = _topk_rows(s)
        sv.append(vals)
        si.append(idx)
    cand, flat, eid = _pair_candidates(sv, si)
    top, e = _topk_rows(cand, flat, eid)
    ex = jnp.exp(top - top[0:1])
    gate = ex / jnp.sum(ex, axis=0, keepdims=True)
    off = pl.multiple_of(h * PEER_TOPK, PEER_TOPK)
    e_ref[0, pl.ds(off, PEER_TOPK), :] = e
    g_ref[0, pl.ds(off, PEER_TOPK), :] = gate


def _peer_route_kernel(q_ref, keys_ref, e_ref, g_ref):
    def head(h, carry):
        _route_head(q_ref, keys_ref, e_ref, g_ref, h)
        return carry

    lax.fori_loop(0, PEER_HEADS, head, 0)


def _peer_route_call(qp, subkeys, tb, n_part):
    nq = qp.shape[1]
    nblk = n_part // tb
    nsel = PEER_HEADS * PEER_TOPK
    out = pl.BlockSpec((1, nsel, tb), lambda i: (i, 0, 0))
    return pl.pallas_call(
        _peer_route_kernel,
        grid=(nblk,),
        in_specs=[pl.BlockSpec((tb, nq), lambda i: (i, 0)),
                  pl.BlockSpec(subkeys.shape, lambda i: (0, 0, 0))],
        out_specs=[out, out],
        out_shape=[jax.ShapeDtypeStruct((nblk, nsel, tb), jnp.int32),
                   jax.ShapeDtypeStruct((nblk, nsel, tb), F32)],
        compiler_params=_cparams(1),
        name="peer_route",
    )(qp, subkeys)


PACK_ROWS = 4
NSEL = PEER_HEADS * PEER_TOPK


def _pack_table(tab):
    n, d = tab.shape
    bits = lax.bitcast_convert_type(tab.astype(BF16), jnp.uint16).astype(jnp.uint32)
    word = bits[:, :d // 2] | (bits[:, d // 2:] << 16)
    return lax.bitcast_convert_type(word, jnp.int32).reshape(n * PACK_ROWS, LANES)


PEER_PARTS = 8
TOK_GROUP = 16
OUT_GROUP = 32

def _gather_group(idx_ref, tab_ref, tile_ref, t0, group=None):
    group = group or TOK_GROUP
    tok_idx = [idx_ref.at[pl.ds(pl.multiple_of((t0 + u) * NSEL, NSEL), NSEL)] for u in range(group)]
    for mi in range(NSEL):
        for u in range(group):
            i = pl.multiple_of(tok_idx[u][mi], PACK_ROWS)
            tile_ref[u, pl.ds(PACK_ROWS * mi, PACK_ROWS), :] = tab_ref[pl.ds(i, PACK_ROWS), :]


def _packed_rows(tile_ref, u):
    g = jnp.concatenate([tile_ref[u, pl.ds(c, NSEL, stride=PACK_ROWS), :] for c in range(PACK_ROWS)], axis=1)
    return pltpu.bitcast(g, BF16)


def _split_hi_lo(x):
    hi = x.astype(BF16).astype(F32)
    return hi, x - hi


ROUTE_TB = 128


def _peer_act_kernel(idx_ref, u_ref, gate_ref, tab_ref, *refs, fuse_route):
    if fuse_route:
        q_ref, keys_ref, w_ref, e_ref, g_ref, tile_ref, act_ref = refs
    else:
        w_ref, tile_ref, act_ref = refs
    half = D_MODEL // 2
    sub = lax.broadcasted_iota(jnp.int32, (8, half), 0)
    even = (lax.broadcasted_iota(jnp.int32, (1, 2 * NSEL), 1) % 2) == 0
    groups_per_head = ROUTE_TB // (PEER_HEADS * TOK_GROUP)

    def step(h, carry):
        if fuse_route:
            _route_head(q_ref, keys_ref, e_ref, g_ref, h)
        for gi in range(groups_per_head):
            t0 = (h * groups_per_head + gi) * TOK_GROUP
            _gather_group(idx_ref, tab_ref, tile_ref.at[gi], t0)
            for u in range(TOK_GROUP):
                hi, lo = _split_hi_lo(u_ref[pl.ds(t0 + u, 1), :])
                lhs = jnp.where(sub == 0, hi[:, :half], jnp.where(sub == 1, hi[:, half:],
                      jnp.where(sub == 2, lo[:, :half], jnp.where(sub == 3, lo[:, half:], 0.0))))
                r = lax.dot_general(lhs.astype(BF16), _packed_rows(tile_ref.at[gi], u), (NT, ((), ())),
                                    preferred_element_type=F32)
                act_ref[pl.ds(t0 + u, 1), :] = jnp.where(even, r[0:1] + r[2:3], r[1:2] + r[3:4])
        return carry

    lax.fori_loop(0, PEER_HEADS, step, 0)
    part = act_ref[...]
    lane_even = (lax.broadcasted_iota(jnp.int32, part.shape, 1) % 2) == 0
    act = part + jnp.where(lane_even, pltpu.roll(part, 2 * NSEL - 1, 1), pltpu.roll(part, 1, 1))
    w_ref[...] = gate_ref[...] * (0.5 * act * (1.0 + lax.erf(act * math.sqrt(0.5))))


def _peer_act_call(idx, u2, gate2, tab, row0, n_part, route=None):
    d = u2.shape[1]
    tb = ROUTE_TB
    b0 = row0 // tb
    groups_per_head = tb // (PEER_HEADS * TOK_GROUP)
    in_specs = [pl.BlockSpec((tb * NSEL,), lambda i: (i,), memory_space=pltpu.SMEM),
                pl.BlockSpec((tb, d), lambda i: (i + b0, 0)),
                pl.BlockSpec((tb, 2 * NSEL), lambda i: (i, 0)),
                pl.BlockSpec(memory_space=pltpu.VMEM)]
    out_specs = [pl.BlockSpec((tb, 2 * NSEL), lambda i: (i, 0))]
    out_shape = [jax.ShapeDtypeStruct((n_part, 2 * NSEL), F32)]
    args = [idx, u2, gate2, tab]
    if route is not None:
        qp, subkeys, q_row0 = route
        qb0 = q_row0 // tb
        in_specs += [pl.BlockSpec((tb, qp.shape[1]), lambda i: (i + qb0, 0)),
                     pl.BlockSpec(subkeys.shape, lambda i: (0, 0, 0))]
        rout = pl.BlockSpec((1, NSEL, tb), lambda i: (i, 0, 0))
        out_specs += [rout, rout]
        out_shape += [jax.ShapeDtypeStruct((n_part // tb, NSEL, tb), jnp.int32),
                      jax.ShapeDtypeStruct((n_part // tb, NSEL, tb), F32)]
        args += [qp, subkeys]
    return pl.pallas_call(
        functools.partial(_peer_act_kernel, fuse_route=route is not None),
        grid=(n_part // tb,),
        in_specs=in_specs,
        out_specs=out_specs,
        out_shape=out_shape,
        scratch_shapes=[pltpu.VMEM((groups_per_head, TOK_GROUP, PACK_ROWS * NSEL, LANES), jnp.int32),
                        pltpu.VMEM((tb, 2 * NSEL), F32)],
        compiler_params=_cparams(1),
        name="peer_act_route" if route is not None else "peer_act",
    )(*args)


def _peer_out_kernel(idx_ref, w_ref, x1_ref, mod_ref, l2w_ref, l2b_ref, tab_ref, x2_ref, tile_ref, acc_ref,
                     *, alpha):
    tb = w_ref.shape[0]
    half = D_MODEL // 2
    sub = lax.broadcasted_iota(jnp.int32, (8, 2 * NSEL), 0)
    even = (lax.broadcasted_iota(jnp.int32, (8, 2 * NSEL), 1) % 2) == 0

    def group(gi, carry):
        t0 = gi * OUT_GROUP
        _gather_group(idx_ref, tab_ref, tile_ref, t0, OUT_GROUP)
        for u in range(OUT_GROUP):
            hi, lo = _split_hi_lo(w_ref[pl.ds(t0 + u, 1), :])
            lhs = jnp.where((sub == 0) & even, hi, jnp.where((sub == 1) & ~even, hi,
                  jnp.where((sub == 2) & even, lo, jnp.where((sub == 3) & ~even, lo, 0.0))))
            r = jnp.dot(lhs.astype(BF16), _packed_rows(tile_ref, u), preferred_element_type=F32)
            acc_ref[pl.ds(t0 + u, 1), 0:half] = r[0:1] + r[2:3]
            acc_ref[pl.ds(t0 + u, 1), half:D_MODEL] = r[1:2] + r[3:4]
        return carry

    lax.fori_loop(0, tb // OUT_GROUP, group, 0)
    m = mod_ref[0, 0]
    x2_ref[...] = _ln(alpha * x1_ref[...] + m[5:6] * acc_ref[...]) * l2w_ref[...] + l2b_ref[...]


def _peer_out_call(idx, w2, x1, mod, ln2_w, ln2_b, tab, alpha, tb, t_len, ctx_len):
    n, d = x1.shape
    seg = lambda i: (((i * tb) % t_len) >= ctx_len).astype(jnp.int32)
    return pl.pallas_call(
        functools.partial(_peer_out_kernel, alpha=alpha),
        grid=(n // tb,),
        in_specs=[pl.BlockSpec((tb * NSEL,), lambda i: (i,), memory_space=pltpu.SMEM),
                  pl.BlockSpec((tb, 2 * NSEL), lambda i: (i, 0)),
                  pl.BlockSpec((tb, d), lambda i: (i, 0)),
                  pl.BlockSpec((1, 1, 6, d), lambda i: ((i * tb) // t_len, seg(i), 0, 0)),
                  pl.BlockSpec((1, d), lambda i: (0, 0)),
                  pl.BlockSpec((1, d), lambda i: (0, 0)),
                  pl.BlockSpec(memory_space=pltpu.VMEM)],
        out_specs=pl.BlockSpec((tb, d), lambda i: (i, 0)),
        out_shape=jax.ShapeDtypeStruct((n, d), F32),
        scratch_shapes=[pltpu.VMEM((OUT_GROUP, PACK_ROWS * NSEL, LANES), jnp.int32),
                        pltpu.VMEM((tb, d), F32)],
        compiler_params=_cparams(1),
        name="peer_out",
    )(idx, w2, x1, mod, ln2_w.reshape(1, d), ln2_b.reshape(1, d), tab)


def _num_parts(n_tok):
    return next(p for p in range(PEER_PARTS, 0, -1) if n_tok % (p * ROUTE_TB) == 0)


def _peer_ffn(qp, u2, x1, mod, subkeys, peer_u, peer_v, ln2_w, ln2_b, alpha, t_len, ctx_len):
    n_tok, d = u2.shape
    parts = _num_parts(n_tok)
    n_part = n_tok // parts
    tab_u = _pack_table(peer_u)
    e_t, gate_t = _peer_route_call(qp, subkeys, ROUTE_TB, n_part)
    idx_parts, w2_parts = [], []
    for p in range(parts):
        idx_p = (jnp.swapaxes(e_t, 1, 2).reshape(n_part * NSEL) * PACK_ROWS).astype(jnp.int32)
        gate2_p = jnp.repeat(jnp.swapaxes(gate_t, 1, 2).reshape(n_part, NSEL), 2, axis=1)
        idx_parts.append(idx_p)
        if p + 1 < parts:
            w2_p, e_t, gate_t = _peer_act_call(idx_p, u2, gate2_p, tab_u, p * n_part, n_part,
                                               route=(qp, subkeys, (p + 1) * n_part))
        else:
            w2_p, = _peer_act_call(idx_p, u2, gate2_p, tab_u, p * n_part, n_part)
        w2_parts.append(w2_p)
    idx = jnp.concatenate(idx_parts)
    w2 = jnp.concatenate(w2_parts, axis=0)
    return _peer_out_call(idx, w2, x1, mod, ln2_w, ln2_b, _pack_table(peer_v), alpha, ROUTE_TB, t_len, ctx_len)


def _rope_tables(n_rows, ctx_len):
    row = jnp.repeat(jnp.arange(n_rows), GRID_W).astype(F32)
    col = jnp.tile(jnp.arange(GRID_W), n_rows).astype(F32)
    quarter = HEAD_DIM // 4
    inv_freq = ROPE_BASE ** (-2.0 * jnp.arange(quarter, dtype=F32) / (HEAD_DIM // 2))
    ang_r = row[:, None] * inv_freq
    ang_c = col[:, None] * inv_freq
    cr, sr, cc, sc = jnp.cos(ang_r), jnp.sin(ang_r), jnp.cos(ang_c), jnp.sin(ang_c)
    cos = jnp.concatenate([cr, cr, cc, cc], axis=-1)
    sin = jnp.concatenate([-sr, sr, -sc, sc], axis=-1)
    cos = jnp.concatenate([jnp.ones((ctx_len, HEAD_DIM), F32), cos], axis=0)
    sin = jnp.concatenate([jnp.zeros((ctx_len, HEAD_DIM), F32), sin], axis=0)
    reps = QK_COLS // HEAD_DIM
    return jnp.tile(cos, (1, reps)), jnp.tile(sin, (1, reps))


def _swap_cols():
    q = HEAD_DIM // 4
    one = np.concatenate([np.arange(q, 2 * q), np.arange(0, q), np.arange(3 * q, 4 * q), np.arange(2 * q, 3 * q)])
    return np.concatenate([one + HEAD_DIM * i for i in range(QK_COLS // HEAD_DIM)])


def kernel(x, c, ctx, c_ctx, ada_w, ada_b, w_in, rwkv_conv, rwkv_w0, rwkv_w2, rwkv_a0, rwkv_a2, rwkv_g2, rwkv_k_k, rwkv_k_a, rwkv_r_k, rwkv_ln_w, rwkv_ln_b, diff_lambda, diff_subln_w, hgrn_lb_logits, hgrn_norm_w, w_out, ln1_w, ln1_b, peer_wq, peer_subkeys, peer_u, peer_v, ln2_w, ln2_b):
    bsz, seq, d = x.shape
    ctx_len = ctx.shape[1]
    depth = w_in.shape[0]
    t_len = ctx_len + seq
    tm = 256 if ctx_len % 256 == 0 else 128
    assert ctx_len % tm == 0 and seq % tm == 0 and seq % GRID_W == 0
    ctx_blocks = ctx_len // tm
    alpha = (2.0 * depth) ** 0.25

    consts = _scan_consts()
    hsum = jnp.asarray((np.arange(A_WIDTH)[:, None] // HEAD_DIM == np.arange(A_WIDTH)[None, :] // HEAD_DIM)
                       .astype(np.float32))
    cos, sin = _rope_tables(seq // GRID_W, ctx_len)
    swap = _swap_cols()

    lb_p = jax.nn.softmax(hgrn_lb_logits.astype(F32), axis=0)
    lower_bounds = jnp.cumsum(lb_p, axis=0) - lb_p[0]

    n_cond = 8 * ((bsz + 1 + 7) // 8)
    cc = jnp.zeros((n_cond, d), F32).at[0].set(c_ctx).at[1:1 + bsz].set(c)
    mods = _ada_call(cc, ada_w, ada_b)

    xs = jnp.concatenate([ctx, x], axis=1)
    for l in range(depth):
        last = l == depth - 1
        m = mods[l].reshape(n_cond, 6, d)
        mod = jnp.stack([jnp.broadcast_to(m[0], (bsz, 6, d)), m[1:1 + bsz]], axis=1)
        q_cols = w_in[l][:, A_COLS:A_COLS + QK_COLS]
        k_cols = w_in[l][:, A_COLS + QK_COLS:A_COLS + 2 * QK_COLS]
        w_ext = jnp.concatenate([w_in[l], q_cols[:, swap], k_cols[:, swap]], axis=1).astype(BF16)
        P = dict(rwkv_conv=rwkv_conv[l], rwkv_w0=rwkv_w0[l], rwkv_w2=rwkv_w2[l], rwkv_a0=rwkv_a0[l],
                 rwkv_a2=rwkv_a2[l], rwkv_g2=rwkv_g2[l], rwkv_k_k=rwkv_k_k[l], rwkv_k_a=rwkv_k_a[l],
                 rwkv_r_k=rwkv_r_k[l], rwkv_ln_w=rwkv_ln_w[l], rwkv_ln_b=rwkv_ln_b[l],
                 hgrn_norm_w=hgrn_norm_w[l], w_out_bf16=w_out[l].astype(BF16),
                 ln1_w=ln1_w[l], ln1_b=ln1_b[l], peer_wq_bf16=peer_wq[l].astype(BF16))

        oa, q, k, v, oc = _inproj_call(xs, mod, w_ext, cos, sin, tm, ctx_blocks)
        r, k2, vv, avec, bvec, lw, g, bonus = _rwkv_prep_call(oa, P, hsum, tm, ctx_blocks)
        rw_o = _rwkv_scan_call(r, k2, vv, avec, bvec, lw, consts, ctx_len // CHUNK)
        hg_o = _hgrn_scan_call(oc, lower_bounds[l], consts, ctx_len // CHUNK)
        lam_init = 0.8 - 0.6 * math.exp(-0.3 * l)
        row0 = ctx_blocks if last else 0
        t_out = t_len - row0 * tm
        yb = _attn_call(q, k, v, diff_lambda[l], diff_subln_w[l], lam_init, tm, ctx_len, skip_ctx=last)
        x1, u2, qp = _outproj_call(xs, mod, rw_o, g, bonus, yb, hg_o, oc, hsum, P, alpha, tm, ctx_blocks, row0)

        x2 = _peer_ffn(qp.reshape(bsz * t_out, -1), u2.reshape(bsz * t_out, d), x1.reshape(bsz * t_out, d), mod,
                       peer_subkeys[l], peer_u[l], peer_v[l], ln2_w[l], ln2_b[l], alpha, t_out,
                       ctx_len - row0 * tm)
        xs = x2.reshape(bsz, t_out, d)
    return xs
```

```python
import functools
import math

import numpy as np
import jax
import jax.numpy as jnp
from jax import lax
from jax.experimental import pallas as pl
from jax.experimental.pallas import tpu as pltpu

F32 = jnp.float32
BF16 = jnp.bfloat16
HIGHEST = lax.Precision.HIGHEST

D_MODEL = 1024
GRID_W = 64
HEAD_DIM = 64
A_WIDTH = 256
A_DECAY_LORA = 32
A_ICLR_LORA = 32
A_GATE_LORA = 64
B_WIDTH = 512
B_HEADS = 4
B_V_DIM = 128
C_WIDTH = 256
ROPE_BASE = 10000.0
MIN_FORGET = 1e-30
PEER_HEADS = 8
PEER_NKEYS = 128
PEER_TOPK = 16
PEER_QDIM = 256
LN_EPS = 1e-5
RWKV_GN_EPS = 64e-5

A_COLS = 3 * A_WIDTH + A_DECAY_LORA + A_ICLR_LORA + A_GATE_LORA
QK_COLS = 2 * B_HEADS * HEAD_DIM
C_COLS = 5 * C_WIDTH
IN_WIDTH = A_COLS + 3 * QK_COLS + C_COLS
SCAN_HEADS = 4
CHUNK = 64
STACK = SCAN_HEADS * CHUNK
SCAN_BATCH = 2
LANES = 128
VMEM_LIMIT = 56 * 1024 * 1024


def _cparams(n_axes, vmem=None):
    return pltpu.CompilerParams(dimension_semantics=("arbitrary",) * n_axes,
                                vmem_limit_bytes=vmem or VMEM_LIMIT)


def _ln(x):
    xc = x - jnp.mean(x, axis=-1, keepdims=True)
    return xc * lax.rsqrt(jnp.mean(xc * xc, axis=-1, keepdims=True) + LN_EPS)


def _mm(a, b, dims=((1,), (0,)), exact=False):
    dn = (dims, ((), ()))
    if exact is True:
        return lax.dot_general(a, b, dn, precision=HIGHEST, preferred_element_type=F32)
    dot = lambda p, q: lax.dot_general(p, q, dn, preferred_element_type=F32)
    a_hi = a.astype(BF16)
    b_hi = b.astype(BF16)
    if exact == "rhs3":
        r1 = b - b_hi.astype(F32)
        b_mid = r1.astype(BF16)
        b_lo = (r1 - b_mid.astype(F32)).astype(BF16)
        return dot(a_hi, b_hi) + (dot(a_hi, b_mid) + dot(a_hi, b_lo))
    if exact == "lhs3":
        r1 = a - a_hi.astype(F32)
        a_mid = r1.astype(BF16)
        a_lo = (r1 - a_mid.astype(F32)).astype(BF16)
        return dot(a_hi, b_hi) + (dot(a_mid, b_hi) + dot(a_lo, b_hi))
    if exact == "x3":
        a_lo = (a - a_hi.astype(F32)).astype(BF16)
        b_lo = (b - b_hi.astype(F32)).astype(BF16)
        return dot(a_hi, b_hi) + (dot(a_hi, b_lo) + dot(a_lo, b_hi))
    return dot(a_hi, b_hi)


NT = ((1,), (1,))
TN = ((0,), (0,))


def _ada_kernel(c_ref, w_ref, b_ref, o_ref):
    c = c_ref[...]
    s = c * jax.nn.sigmoid(c)
    o_ref[...] = _mm(s, w_ref[...], exact=True) + b_ref[...]


def _ada_call(cc, ada_w, ada_b):
    depth, d, n = ada_w.shape
    bn = 512
    return pl.pallas_call(
        _ada_kernel,
        grid=(depth, n // bn),
        in_specs=[pl.BlockSpec(cc.shape, lambda l, j: (0, 0)),
                  pl.BlockSpec((None, d, bn), lambda l, j: (l, 0, j)),
                  pl.BlockSpec((None, 1, bn), lambda l, j: (l, 0, j))],
        out_specs=pl.BlockSpec((None, cc.shape[0], bn), lambda l, j: (l, 0, j)),
        out_shape=jax.ShapeDtypeStruct((depth, cc.shape[0], n), F32),
        compiler_params=_cparams(2),
        name="ada_mod",
    )(cc, ada_w, ada_b.reshape(depth, 1, n))


def _inproj_kernel(x_ref, mod_ref, w_ref, cos_ref, sin_ref, oa_ref, q_ref, k_ref, v_ref, oc_ref):
    m = mod_ref[0, 0]
    u = _ln(x_ref[0]) * (1.0 + m[1:2]) + m[0:1]
    h = _mm(u, w_ref[...])
    oa_ref[0] = h[:, 0:A_COLS]
    cos = cos_ref[...]
    sin = sin_ref[...]
    q0 = A_COLS
    k0 = q0 + QK_COLS
    v0 = k0 + QK_COLS
    c0 = v0 + QK_COLS
    qs0 = IN_WIDTH
    ks0 = qs0 + QK_COLS
    q = (h[:, q0:k0] * cos + h[:, qs0:ks0] * sin) * (math.log2(math.e) * HEAD_DIM ** -0.5)
    k = h[:, k0:v0] * cos + h[:, ks0:ks0 + QK_COLS] * sin
    q_ref[0] = q.astype(BF16)
    k_ref[0] = k.astype(BF16)
    v_ref[0] = h[:, v0:c0].astype(BF16)
    oc_ref[0] = h[:, c0:IN_WIDTH]


def _inproj_call(x, mod, w_ext, cos, sin, tm, ctx_blocks):
    b, t, d = x.shape
    n_ext = w_ext.shape[1]
    tok = lambda width: pl.BlockSpec((1, tm, width), lambda i, j: (i, j, 0))
    return pl.pallas_call(
        _inproj_kernel,
        grid=(b, t // tm),
        in_specs=[tok(d),
                  pl.BlockSpec((1, 1, 6, d), lambda i, j: (i, (j >= ctx_blocks).astype(jnp.int32), 0, 0)),
                  pl.BlockSpec((d, n_ext), lambda i, j: (0, 0)),
                  pl.BlockSpec((tm, QK_COLS), lambda i, j: (j, 0)),
                  pl.BlockSpec((tm, QK_COLS), lambda i, j: (j, 0))],
        out_specs=[tok(A_COLS), tok(QK_COLS), tok(QK_COLS), tok(QK_COLS), tok(C_COLS)],
        out_shape=[jax.ShapeDtypeStruct((b, t, A_COLS), F32),
                   jax.ShapeDtypeStruct((b, t, QK_COLS), BF16),
                   jax.ShapeDtypeStruct((b, t, QK_COLS), BF16),
                   jax.ShapeDtypeStruct((b, t, QK_COLS), BF16),
                   jax.ShapeDtypeStruct((b, t, C_COLS), F32)],
        compiler_params=_cparams(2),
        name="in_proj",
    )(x, mod, w_ext, cos, sin)


def _rwkv_prep_kernel(cur_ref, prev_ref, next_ref, conv_ref, w0_ref, w2_ref, a0_ref, a2_ref, g2_ref,
                      kk_ref, ka_ref, rk_ref, hsum_ref,
                      r_o, k_o, v_o, a_o, b_o, lw_o, g_o, bonus_o, *, ctx_blocks, n_blocks):
    j = pl.program_id(1)
    cur = cur_ref[0]
    tm = cur.shape[0]
    w3 = 3 * A_WIDTH
    rkv = cur[:, 0:w3]
    has_prev = jnp.logical_and(j != 0, j != ctx_blocks)
    has_next = jnp.logical_and(j != ctx_blocks - 1, j != n_blocks - 1)
    prev_row = jnp.where(has_prev, prev_ref[0][7:8, 0:w3], 0.0)
    next_row = jnp.where(has_next, next_ref[0][0:1, 0:w3], 0.0)
    row = lax.broadcasted_iota(jnp.int32, (tm, w3), 0)
    xm1 = jnp.where(row == 0, prev_row, pltpu.roll(rkv, 1, 0))
    xp1 = jnp.where(row == tm - 1, next_row, pltpu.roll(rkv, tm - 1, 0))
    cw = conv_ref[...]
    conv = cw[0:1] * xm1 + cw[1:2] * rkv + cw[2:3] * xp1
    r = conv[:, 0:A_WIDTH]
    k = conv[:, A_WIDTH:2 * A_WIDTH]
    v = conv[:, 2 * A_WIDTH:w3]
    o = w3
    w_lo = cur[:, o:o + A_DECAY_LORA]
    a_lo = cur[:, o + A_DECAY_LORA:o + A_DECAY_LORA + A_ICLR_LORA]
    g_lo = cur[:, o + A_DECAY_LORA + A_ICLR_LORA:A_COLS]
    a = jax.nn.sigmoid(a0_ref[...] + _mm(a_lo, a2_ref[...], exact=True))
    g = _mm(jax.nn.sigmoid(g_lo), g2_ref[...], exact=True)
    hsum = hsum_ref[...]
    kk = k * kk_ref[...]
    ss = _mm(kk * kk, hsum, exact="lhs3")
    kk = kk * lax.rsqrt(jnp.maximum(ss, 1e-24))
    k2 = k * (1.0 + (a - 1.0) * ka_ref[...])
    tw = jnp.tanh(w_lo)
    w0 = w0_ref[...]
    for d in range(2):
        wl = w0[d:d + 1] + _mm(tw, w2_ref[d], exact=True)
        lw_o[d, 0] = -math.exp(-0.5) * jax.nn.sigmoid(wl)
    r_o[0] = r
    k_o[0] = k2
    v_o[0] = v
    a_o[0] = -kk
    b_o[0] = kk * a
    g_o[0] = g
    bonus_o[0] = _mm(r * k2 * rk_ref[...], hsum, exact="lhs3") * v


def _rwkv_prep_call(oa, P, hsum, tm, ctx_blocks):
    b, t, _ = oa.shape
    n_blocks = t // tm
    per8 = tm // 8
    last8 = t // 8 - 1
    full = lambda arr: pl.BlockSpec(arr.shape, lambda i, j: (0,) * arr.ndim)
    tok = pl.BlockSpec((1, tm, A_WIDTH), lambda i, j: (i, j, 0))
    row = lambda p: p.reshape(1, -1)
    params = [P['rwkv_conv'], P['rwkv_w0'], P['rwkv_w2'], row(P['rwkv_a0']), P['rwkv_a2'], P['rwkv_g2'],
              row(P['rwkv_k_k']), row(P['rwkv_k_a']), row(P['rwkv_r_k']), hsum]
    shp = jax.ShapeDtypeStruct((b, t, A_WIDTH), F32)
    return pl.pallas_call(
        functools.partial(_rwkv_prep_kernel, ctx_blocks=ctx_blocks, n_blocks=n_blocks),
        grid=(b, n_blocks),
        in_specs=[pl.BlockSpec((1, tm, A_COLS), lambda i, j: (i, j, 0)),
                  pl.BlockSpec((1, 8, A_COLS), lambda i, j: (i, jnp.maximum(j * per8 - 1, 0), 0)),
                  pl.BlockSpec((1, 8, A_COLS), lambda i, j: (i, jnp.minimum((j + 1) * per8, last8), 0))]
                 + [full(p) for p in params],
        out_specs=[tok, tok, tok, tok, tok,
                   pl.BlockSpec((2, 1, tm, A_WIDTH), lambda i, j: (0, i, j, 0)), tok, tok],
        out_shape=[shp, shp, shp, shp, shp, jax.ShapeDtypeStruct((2, b, t, A_WIDTH), F32), shp, shp],
        compiler_params=_cparams(2),
        name="rwkv_prep",
    )(oa, oa, oa, *params)


def _scan_consts():
    c = CHUNK
    t = np.arange(c)
    tri = np.zeros((2, c, c), np.float32)
    tri[0] = (t[None, :] <= t[:, None])
    tri[1] = (t[None, :] >= t[:, None])
    st = np.arange(STACK)
    same_head = (st[:, None] // c) == (st[None, :] // c)
    tt = st[:, None] % c
    ss = st[None, :] % c
    strict = np.stack([same_head & (ss < tt), same_head & (ss > tt)]).astype(np.float32)
    incl = np.stack([same_head & (ss <= tt), same_head & (ss >= tt)]).astype(np.float32)
    head_mask = ((st[:, None] // c) == (np.arange(SCAN_HEADS * HEAD_DIM)[None, :] // HEAD_DIM)).astype(np.float32)
    n_lv = int(math.log2(c))
    seg = np.zeros((2, (n_lv + 1) * c, c), np.float32)
    seg[0, :c] = tri[0]
    seg[1, :c] = tri[1]
    level = -np.ones((2, STACK, STACK), np.int32)
    for d in range(2):
        level[d][same_head & (tt == ss)] = 0
    for l in range(1, n_lv + 1):
        n = 2 ** l
        for ti in range(c):
            s0 = (ti // n) * n
            m = s0 + n // 2 - 1
            hh = s0 + n // 2
            if ti > m:
                seg[0, l * c + ti, m + 1:ti + 1] = 1.0
            else:
                seg[0, l * c + ti, ti + 1:m + 1] = 1.0
            if ti < hh:
                seg[1, l * c + ti, ti:hh] = 1.0
            else:
                seg[1, l * c + ti, hh:ti] = 1.0
        same_blk = (tt // n) == (ss // n)
        t_second = (tt % n) >= n // 2
        s_second = (ss % n) >= n // 2
        level[0][same_head & same_blk & t_second & ~s_second] = l
        level[1][same_head & same_blk & ~t_second & s_second] = l
    return dict(tri=jnp.asarray(tri), strict=jnp.asarray(strict), incl=jnp.asarray(incl),
                head_mask=jnp.asarray(head_mask), seg=jnp.asarray(seg), level=jnp.asarray(level))


def _tile4(x):
    return jnp.concatenate([x] * SCAN_HEADS, axis=0)


def _fold4(x):
    c = CHUNK
    return x[0:c] + x[c:2 * c] + x[2 * c:3 * c] + x[3 * c:4 * c]


def _chunk_index(d, j, nc_ctx, nc):
    if d == 0:
        return j
    return jnp.where(j < nc_ctx, nc_ctx - 1 - j, nc - 1 - (j - nc_ctx))


def _rwkv_chunk(tri, strict, incl, hm, r, k, v, a, b, lw, z, mm, mmi):
    cum = _mm(tri, lw, exact="rhs3")
    tot = jnp.sum(lw, axis=0, keepdims=True)
    e_neg = jnp.exp(-cum)
    e_end = jnp.exp(tot - cum)
    a_sm = _tile4(a * jnp.exp(cum - lw)) * hm
    r_sm = _tile4(r * jnp.exp(cum)) * hm
    v_sm = _tile4(v) * hm
    b_rep = _tile4(b * e_neg)
    k_rep = _tile4(k * e_neg)
    zero = jnp.zeros((STACK, STACK), F32)
    n_ab = jnp.where(strict, mm(a_sm, b_rep, NT), zero)
    n_ak = jnp.where(strict, mm(a_sm, k_rep, NT), zero)
    n_rb = jnp.where(incl, mm(r_sm, b_rep, NT), zero)
    n_rk = jnp.where(incl, mm(r_sm, k_rep, NT), zero)
    rows = lax.broadcasted_iota(jnp.int32, (STACK, STACK), 0)
    cols = lax.broadcasted_iota(jnp.int32, (STACK, STACK), 1)
    inv = jnp.where(rows == cols, 1.0, 0.0) + n_ab
    npow = n_ab
    for _ in range(int(math.log2(CHUNK)) - 1):
        npow = mmi(npow, npow)
        inv = inv + mmi(inv, npow)
    u_sm = mm(inv, mm(a_sm, z, NT) + mm(n_ak, v_sm))
    o_sm = mm(r_sm, z, NT) + mm(n_rb, u_sm) + mm(n_rk, v_sm)
    z_new = z * jnp.exp(tot) + mm(u_sm, _tile4(b * e_end) * hm, TN) + mm(v_sm, _tile4(k * e_end) * hm, TN)
    return _fold4(o_sm), z_new


def _rwkv_scan_kernel(tri_ref, strict_ref, incl_ref, hm_ref, *refs, exact, inv_exact):
    ins, (of_ref, ob_ref, z_ref) = refs[:12], refs[12:]

    @pl.when(pl.program_id(1) == 0)
    def _():
        z_ref[...] = jnp.zeros_like(z_ref)

    mm = functools.partial(_mm, exact=exact)
    mmi = functools.partial(_mm, exact=inv_exact)
    hm = hm_ref[...]
    for bb in range(SCAN_BATCH):
        for d, o_ref in enumerate((of_ref, ob_ref)):
            r, k, v, a, b = (ref[bb] for ref in ins[6 * d:6 * d + 5])
            o, z_new = _rwkv_chunk(tri_ref[d], strict_ref[d] > 0.0, incl_ref[d] > 0.0, hm, r, k, v, a, b,
                                   ins[6 * d + 5][0, bb], z_ref[2 * bb + d], mm, mmi)
            o_ref[bb] = o
            z_ref[2 * bb + d] = z_new


def _rwkv_scan_call(r, k, v, a, bvec, lw, consts, nc_ctx, exact=False, inv_exact=False):
    b, t, w = r.shape
    nc = t // CHUNK
    full = lambda arr: pl.BlockSpec(arr.shape, lambda i, j: (0,) * arr.ndim)
    specs, args = [], []
    for d in range(2):
        cidx = functools.partial(_chunk_index, d, nc_ctx=nc_ctx, nc=nc)
        tok = pl.BlockSpec((SCAN_BATCH, CHUNK, w), lambda i, j, cidx=cidx: (i, cidx(j), 0))
        specs += [tok] * 5 + [pl.BlockSpec((1, SCAN_BATCH, CHUNK, w),
                                           lambda i, j, cidx=cidx, d=d: (d, i, cidx(j), 0))]
        args += [r, k, v, a, bvec, lw]
    out_specs = [pl.BlockSpec((SCAN_BATCH, CHUNK, w), lambda i, j, d=d: (i, _chunk_index(d, j, nc_ctx, nc), 0))
                 for d in range(2)]
    cs = [consts['tri'], consts['strict'], consts['incl'], consts['head_mask']]
    return pl.pallas_call(
        functools.partial(_rwkv_scan_kernel, exact=exact, inv_exact=inv_exact),
        grid=(b // SCAN_BATCH, nc),
        in_specs=[full(c) for c in cs] + specs,
        out_specs=out_specs,
        out_shape=[jax.ShapeDtypeStruct((b, t, w), F32)] * 2,
        scratch_shapes=[pltpu.VMEM((2 * SCAN_BATCH, STACK, STACK), F32)],
        compiler_params=_cparams(2),
        name="rwkv_scan",
    )(*cs, *args)


def _hgrn_chunk(seg, level, hm, lb, qr, zf, v, z, mm):
    q = qr * jax.nn.sigmoid(qr)
    f = lb + (1.0 - lb) * jax.nn.sigmoid(zf)
    logf = jnp.log(jnp.maximum(f, MIN_FORGET))
    k = (1.0 - lb) * jax.nn.sigmoid(-zf)
    segs = _mm(seg, logf, exact="rhs3")
    cum = segs[0:CHUNK]
    tot = jnp.sum(logf, axis=0, keepdims=True)
    attn = jnp.where(level == 0, mm(_tile4(q) * hm, _tile4(k), NT), 0.0)
    for l in range(1, int(math.log2(CHUNK)) + 1):
        e = jnp.exp(segs[l * CHUNK:(l + 1) * CHUNK])
        attn = jnp.where(level == l, mm(_tile4(q * e) * hm, _tile4(k * e), NT), attn)
    v_sm = _tile4(v) * hm
    o_sm = mm(_tile4(q * jnp.exp(cum)) * hm, z, NT) + mm(attn, v_sm)
    z_new = z * jnp.exp(tot) + mm(v_sm, _tile4(k * jnp.exp(tot - cum)) * hm, TN)
    return _fold4(o_sm), z_new


def _hgrn_scan_kernel(seg_ref, level_ref, hm_ref, lb_ref, *refs, exact):
    ins, (of_ref, ob_ref, z_ref) = refs[:6], refs[6:]

    @pl.when(pl.program_id(1) == 0)
    def _():
        z_ref[...] = jnp.zeros_like(z_ref)

    mm = functools.partial(_mm, exact=exact)
    hm = hm_ref[...]
    lb = lb_ref[...]
    for bb in range(SCAN_BATCH):
        for d, o_ref in enumerate((of_ref, ob_ref)):
            qr, zf, v = (ref[bb] for ref in ins[3 * d:3 * d + 3])
            o, z_new = _hgrn_chunk(seg_ref[d], level_ref[d], hm, lb, qr, zf, v, z_ref[2 * bb + d], mm)
            o_ref[bb] = o
            z_ref[2 * bb + d] = z_new


def _hgrn_scan_call(oc, lb, consts, nc_ctx, exact=False):
    b, t, _ = oc.shape
    w = C_WIDTH
    nc = t // CHUNK
    full = lambda arr: pl.BlockSpec(arr.shape, lambda i, j: (0,) * arr.ndim)
    specs = []
    for d in range(2):
        cidx = functools.partial(_chunk_index, d, nc_ctx=nc_ctx, nc=nc)
        specs += [pl.BlockSpec((SCAN_BATCH, CHUNK, w), lambda i, j, cidx=cidx, col=col: (i, cidx(j), col))
                  for col in (0, 1 + d, 3)]
    out_specs = [pl.BlockSpec((SCAN_BATCH, CHUNK, w), lambda i, j, d=d: (i, _chunk_index(d, j, nc_ctx, nc), 0))
                 for d in range(2)]
    cs = [consts['seg'], consts['level'], consts['head_mask'], lb.reshape(1, w)]
    return pl.pallas_call(
        functools.partial(_hgrn_scan_kernel, exact=exact),
        grid=(b // SCAN_BATCH, nc),
        in_specs=[full(c) for c in cs] + specs,
        out_specs=out_specs,
        out_shape=[jax.ShapeDtypeStruct((b, t, w), F32)] * 2,
        scratch_shapes=[pltpu.VMEM((2 * SCAN_BATCH, STACK, STACK), F32)],
        compiler_params=_cparams(2),
        name="hgrn_scan",
    )(*cs, *([oc] * 6))


def _attn_kernel(lam_ref, w_ref, q_ref, k_ref, v_ref, o_ref, *, lam_init, ctx_len, ctx_blocks):
    j = pl.program_id(2)
    ll = lam_ref[...]
    lam = (jnp.exp(jnp.sum(ll[0:1] * ll[1:2], axis=-1, keepdims=True))
           - jnp.exp(jnp.sum(ll[2:3] * ll[3:4], axis=-1, keepdims=True)) + lam_init)
    q = q_ref[0]
    first = lax.broadcasted_iota(jnp.int32, q.shape, 1) < HEAD_DIM
    zq = jnp.zeros_like(q)
    q1 = jnp.where(first, q, zq)
    q2 = jnp.where(first, zq, q)

    def attend(k, v):
        def softmax_parts(qm):
            s = lax.dot_general(qm, k, (NT, ((), ())), preferred_element_type=F32)
            p = jnp.exp2(s - jnp.max(s, axis=-1, keepdims=True))
            return p, 1.0 / jnp.sum(p, axis=-1, keepdims=True)
        p1, i1 = softmax_parts(q1)
        p2, i2 = softmax_parts(q2)
        a = p1 * i1 - p2 * (lam * i2)
        o = jnp.dot(a.astype(BF16), v, preferred_element_type=F32)
        y = o * lax.rsqrt(jnp.mean(o * o, axis=-1, keepdims=True) + LN_EPS)
        o_ref[0] = y * w_ref[...] * (1.0 - lam_init)

    @pl.when(j < ctx_blocks)
    def _():
        attend(k_ref[0, 0:ctx_len], v_ref[0, 0:ctx_len])

    @pl.when(j >= ctx_blocks)
    def _():
        attend(k_ref[0], v_ref[0])


def _attn_call(q, k, v, diff_lambda, subln_w, lam_init, tq, ctx_len, skip_ctx):
    b, t, _ = q.shape
    ctx_blocks = ctx_len // tq
    j0 = ctx_blocks if skip_ctx else 0
    kv = pl.BlockSpec((1, t, B_V_DIM), lambda i, h, j: (i, 0, h))
    qo = pl.BlockSpec((1, tq, B_V_DIM), lambda i, h, j: (i, j + j0, h))
    return pl.pallas_call(
        functools.partial(_attn_kernel, lam_init=lam_init, ctx_len=ctx_len, ctx_blocks=ctx_blocks - j0),
        grid=(b, B_HEADS, t // tq - j0),
        in_specs=[pl.BlockSpec((4, HEAD_DIM), lambda i, h, j: (0, 0)),
                  pl.BlockSpec((1, B_V_DIM), lambda i, h, j: (0, 0)),
                  qo, kv, kv],
        out_specs=qo,
        out_shape=jax.ShapeDtypeStruct((b, t, B_WIDTH), F32),
        compiler_params=_cparams(3),
        name="diff_attn",
    )(diff_lambda, subln_w.reshape(1, B_V_DIM), q, k, v)


def _outproj_kernel(x_ref, mod_ref, of_ref, ob_ref, g_ref, bonus_ref, yb_ref, cf_ref, cb_ref, cg_ref,
                    hsum_ref, lnw_ref, lnb_ref, hw_ref, wout_ref, l1w_ref, l1b_ref, wq_ref,
                    x1_ref, u2_ref, qp_ref, *, alpha):
    m = mod_ref[0, 0]
    hsum = hsum_ref[...]
    inv = 1.0 / HEAD_DIM
    o = of_ref[0] + ob_ref[0]
    oc = o - _mm(o, hsum, exact="lhs3") * inv
    ya = oc * lax.rsqrt(_mm(oc * oc, hsum, exact="lhs3") * inv + RWKV_GN_EPS)
    ya = (ya * lnw_ref[...] + lnb_ref[...] + bonus_ref[0]) * g_ref[0]
    c = cf_ref[0] + cb_ref[0]
    cg = cg_ref[0]
    yc = c * lax.rsqrt(_mm(c * c, hsum, exact="lhs3") * inv + LN_EPS) * hw_ref[...] * (cg * jax.nn.sigmoid(cg))
    y = jnp.concatenate([ya, yb_ref[0], yc], axis=-1)
    proj = _mm(y, wout_ref[...])
    x1 = _ln(alpha * x_ref[0] + m[2:3] * proj) * l1w_ref[...] + l1b_ref[...]
    x1_ref[0] = x1
    u2 = _ln(x1) * (1.0 + m[4:5]) + m[3:4]
    u2_ref[0] = u2
    qp_ref[0] = _mm(u2, wq_ref[...])


def _outproj_call(x, mod, rw_o, rw_g, rw_bonus, yb, hg_o, oc, hsum, P, alpha, tm, ctx_blocks, row0):
    b, t, d = x.shape
    nq = P['peer_wq_bf16'].shape[1]
    t_out = t - row0 * tm
    tok = lambda width, col=0: pl.BlockSpec((1, tm, width), lambda i, j: (i, j + row0, col))
    otok = lambda width: pl.BlockSpec((1, tm, width), lambda i, j: (i, j, 0))
    full = lambda arr: pl.BlockSpec(arr.shape, lambda i, j: (0,) * arr.ndim)
    row = lambda p: p.reshape(1, -1)
    params = [hsum, row(P['rwkv_ln_w']), row(P['rwkv_ln_b']), row(jnp.tile(P['hgrn_norm_w'], SCAN_HEADS)),
              P['w_out_bf16'], row(P['ln1_w']), row(P['ln1_b']), P['peer_wq_bf16']]
    return pl.pallas_call(
        functools.partial(_outproj_kernel, alpha=alpha),
        grid=(b, t_out // tm),
        in_specs=[tok(d),
                  pl.BlockSpec((1, 1, 6, d), lambda i, j: (i, (j + row0 >= ctx_blocks).astype(jnp.int32), 0, 0)),
                  tok(A_WIDTH), tok(A_WIDTH), tok(A_WIDTH), tok(A_WIDTH), tok(B_WIDTH),
                  tok(C_WIDTH), tok(C_WIDTH), tok(C_WIDTH, 4)] + [full(p) for p in params],
        out_specs=[otok(d), otok(d), otok(nq)],
        out_shape=[jax.ShapeDtypeStruct((b, t_out, d), F32), jax.ShapeDtypeStruct((b, t_out, d), F32),
                   jax.ShapeDtypeStruct((b, t_out, nq), F32)],
        compiler_params=_cparams(2),
        name="out_proj",
    )(x, mod, rw_o[0], rw_o[1], rw_g, rw_bonus, yb, hg_o[0], hg_o[1], oc, *params)


def _topk_rows(s, order=None, payload=None):
    if order is None:
        order = lax.broadcasted_iota(jnp.int32, s.shape, 0)
    big = jnp.int32(2 ** 30)
    vals, picks = [], []
    for _ in range(PEER_TOPK):
        m = jnp.max(s, axis=0, keepdims=True)
        idx = jnp.min(jnp.where(s == m, order, big), axis=0, keepdims=True)
        hit = order == idx
        vals.append(m)
        picks.append(idx if payload is None else jnp.max(jnp.where(hit, payload, -1), axis=0, keepdims=True))
        s = jnp.where(hit, -jnp.inf, s)
    return jnp.concatenate(vals, axis=0), jnp.concatenate(picks, axis=0)


def _pair_candidates(sv, si):
    k = PEER_TOPK
    vals, flat, eid = [], [], []
    tokens = sv[0].shape[1]
    row8 = lax.broadcasted_iota(jnp.int32, (8, tokens), 0)
    row16 = lax.broadcasted_iota(jnp.int32, (k, tokens), 0)
    neg = -jnp.inf
    for i in range(8):
        n_valid = k // (i + 1)
        rows, row = (k, row16) if n_valid > 8 else (8, row8)
        v = sv[0][i:i + 1] + sv[1][0:rows]
        vals.append(v if n_valid == rows else jnp.where(row < n_valid, v, neg))
        flat.append(i * k + row)
        eid.append(si[0][i:i + 1] * PEER_NKEYS + si[1][0:rows])
    vals.append(sv[0][8:k] + sv[1][0:1])
    flat.append((row8 + 8) * k)
    eid.append(si[0][8:k] * PEER_NKEYS + si[1][0:1])
    cat = lambda xs: jnp.concatenate(xs, axis=0)
    return cat(vals), cat(flat), cat(eid)


def _route_head(q_ref, keys_ref, e_ref, g_ref, h):
    half = PEER_QDIM // 2
    sv, si = [], []
    for p in range(2):
        start = pl.multiple_of((2 * h + p) * half, half)
        qhp = q_ref[:, pl.ds(start, half)]
        s = _mm(keys_ref[p], qhp, NT, exact=True)
        vals, idx = _topk_rows(s)
        sv.append(vals)
        si.append(idx)
    cand, flat, eid = _pair_candidates(sv, si)
    top, e = _topk_rows(cand, flat, eid)
    ex = jnp.exp(top - top[0:1])
    gate = ex / jnp.sum(ex, axis=0, keepdims=True)
    off = pl.multiple_of(h * PEER_TOPK, PEER_TOPK)
    e_ref[0, pl.ds(off, PEER_TOPK), :] = e
    g_ref[0, pl.ds(off, PEER_TOPK), :] = gate


def _peer_route_kernel(q_ref, keys_ref, e_ref, g_ref):
    def head(h, carry):
        _route_head(q_ref, keys_ref, e_ref, g_ref, h)
        return carry

    lax.fori_loop(0, PEER_HEADS, head, 0)


def _peer_route_call(qp, subkeys, tb, n_part):
    nq = qp.shape[1]
    nblk = n_part // tb
    nsel = PEER_HEADS * PEER_TOPK
    out = pl.BlockSpec((1, nsel, tb), lambda i: (i, 0, 0))
    return pl.pallas_call(
        _peer_route_kernel,
        grid=(nblk,),
        in_specs=[pl.BlockSpec((tb, nq), lambda i: (i, 0)),
                  pl.BlockSpec(subkeys.shape, lambda i: (0, 0, 0))],
        out_specs=[out, out],
        out_shape=[jax.ShapeDtypeStruct((nblk, nsel, tb), jnp.int32),
                   jax.ShapeDtypeStruct((nblk, nsel, tb), F32)],
        compiler_params=_cparams(1),
        name="peer_route",
    )(qp, subkeys)


PACK_ROWS = 4
NSEL = PEER_HEADS * PEER_TOPK


def _pack_table(tab):
    n, d = tab.shape
    bits = lax.bitcast_convert_type(tab.astype(BF16), jnp.uint16).astype(jnp.uint32)
    word = bits[:, :d // 2] | (bits[:, d // 2:] << 16)
    return lax.bitcast_convert_type(word, jnp.int32).reshape(n * PACK_ROWS, LANES)


PEER_PARTS = 16
TOK_GROUP = 16
OUT_GROUP = 32

def _gather_group(idx_ref, tab_ref, tile_ref, t0, group=None):
    group = group or TOK_GROUP
    tok_idx = [idx_ref.at[pl.ds(pl.multiple_of((t0 + u) * NSEL, NSEL), NSEL)] for u in range(group)]
    for mi in range(NSEL):
        for u in range(group):
            i = pl.multiple_of(tok_idx[u][mi], PACK_ROWS)
            tile_ref[u, pl.ds(PACK_ROWS * mi, PACK_ROWS), :] = tab_ref[pl.ds(i, PACK_ROWS), :]


def _packed_rows(tile_ref, u):
    g = jnp.concatenate([tile_ref[u, pl.ds(c, NSEL, stride=PACK_ROWS), :] for c in range(PACK_ROWS)], axis=1)
    return pltpu.bitcast(g, BF16)


def _split_hi_lo(x):
    hi = x.astype(BF16).astype(F32)
    return hi, x - hi


ROUTE_TB = 128


def _peer_act_kernel(idx_ref, u_ref, gate_ref, tab_ref, *refs, fuse_route):
    if fuse_route:
        q_ref, keys_ref, w_ref, e_ref, g_ref, tile_ref, act_ref = refs
    else:
        w_ref, tile_ref, act_ref = refs
    half = D_MODEL // 2
    sub = lax.broadcasted_iota(jnp.int32, (8, half), 0)
    even = (lax.broadcasted_iota(jnp.int32, (1, 2 * NSEL), 1) % 2) == 0
    groups_per_head = ROUTE_TB // (PEER_HEADS * TOK_GROUP)

    def step(h, carry):
        if fuse_route:
            _route_head(q_ref, keys_ref, e_ref, g_ref, h)
        for gi in range(groups_per_head):
            t0 = (h * groups_per_head + gi) * TOK_GROUP
            _gather_group(idx_ref, tab_ref, tile_ref.at[gi], t0)
            for u in range(TOK_GROUP):
                hi, lo = _split_hi_lo(u_ref[pl.ds(t0 + u, 1), :])
                lhs = jnp.where(sub == 0, hi[:, :half], jnp.where(sub == 1, hi[:, half:],
                      jnp.where(sub == 2, lo[:, :half], jnp.where(sub == 3, lo[:, half:], 0.0))))
                r = lax.dot_general(lhs.astype(BF16), _packed_rows(tile_ref.at[gi], u), (NT, ((), ())),
                                    preferred_element_type=F32)
                act_ref[pl.ds(t0 + u, 1), :] = jnp.where(even, r[0:1] + r[2:3], r[1:2] + r[3:4])
        return carry

    lax.fori_loop(0, PEER_HEADS, step, 0)
    part = act_ref[...]
    lane_even = (lax.broadcasted_iota(jnp.int32, part.shape, 1) % 2) == 0
    act = part + jnp.where(lane_even, pltpu.roll(part, 2 * NSEL - 1, 1), pltpu.roll(part, 1, 1))
    w_ref[...] = gate_ref[...] * (0.5 * act * (1.0 + lax.erf(act * math.sqrt(0.5))))


def _peer_act_call(idx, u2, gate2, tab, row0, n_part, route=None):
    d = u2.shape[1]
    tb = ROUTE_TB
    b0 = row0 // tb
    groups_per_head = tb // (PEER_HEADS * TOK_GROUP)
    in_specs = [pl.BlockSpec((tb * NSEL,), lambda i: (i,), memory_space=pltpu.SMEM),
                pl.BlockSpec((tb, d), lambda i: (i + b0, 0)),
                pl.BlockSpec((tb, 2 * NSEL), lambda i: (i, 0)),
                pl.BlockSpec(memory_space=pltpu.VMEM)]
    out_specs = [pl.BlockSpec((tb, 2 * NSEL), lambda i: (i, 0))]
    out_shape = [jax.ShapeDtypeStruct((n_part, 2 * NSEL), F32)]
    args = [idx, u2, gate2, tab]
    if route is not None:
        qp, subkeys, q_row0 = route
        qb0 = q_row0 // tb
        in_specs += [pl.BlockSpec((tb, qp.shape[1]), lambda i: (i + qb0, 0)),
                     pl.BlockSpec(subkeys.shape, lambda i: (0, 0, 0))]
        rout = pl.BlockSpec((1, NSEL, tb), lambda i: (i, 0, 0))
        out_specs += [rout, rout]
        out_shape += [jax.ShapeDtypeStruct((n_part // tb, NSEL, tb), jnp.int32),
                      jax.ShapeDtypeStruct((n_part // tb, NSEL, tb), F32)]
        args += [qp, subkeys]
    return pl.pallas_call(
        functools.partial(_peer_act_kernel, fuse_route=route is not None),
        grid=(n_part // tb,),
        in_specs=in_specs,
        out_specs=out_specs,
        out_shape=out_shape,
        scratch_shapes=[pltpu.VMEM((groups_per_head, TOK_GROUP, PACK_ROWS * NSEL, LANES), jnp.int32),
                        pltpu.VMEM((tb, 2 * NSEL), F32)],
        compiler_params=_cparams(1),
        name="peer_act_route" if route is not None else "peer_act",
    )(*args)


def _peer_out_kernel(idx_ref, w_ref, x1_ref, mod_ref, l2w_ref, l2b_ref, tab_ref, x2_ref, tile_ref, acc_ref,
                     *, alpha):
    tb = w_ref.shape[0]
    half = D_MODEL // 2
    sub = lax.broadcasted_iota(jnp.int32, (8, 2 * NSEL), 0)
    even = (lax.broadcasted_iota(jnp.int32, (8, 2 * NSEL), 1) % 2) == 0

    def group(gi, carry):
        t0 = gi * OUT_GROUP
        _gather_group(idx_ref, tab_ref, tile_ref, t0, OUT_GROUP)
        for u in range(OUT_GROUP):
            hi, lo = _split_hi_lo(w_ref[pl.ds(t0 + u, 1), :])
            lhs = jnp.where((sub == 0) & even, hi, jnp.where((sub == 1) & ~even, hi,
                  jnp.where((sub == 2) & even, lo, jnp.where((sub == 3) & ~even, lo, 0.0))))
            r = jnp.dot(lhs.astype(BF16), _packed_rows(tile_ref, u), preferred_element_type=F32)
            acc_ref[pl.ds(t0 + u, 1), 0:half] = r[0:1] + r[2:3]
            acc_ref[pl.ds(t0 + u, 1), half:D_MODEL] = r[1:2] + r[3:4]
        return carry

    lax.fori_loop(0, tb // OUT_GROUP, group, 0)
    m = mod_ref[0, 0]
    x2_ref[...] = _ln(alpha * x1_ref[...] + m[5:6] * acc_ref[...]) * l2w_ref[...] + l2b_ref[...]


def _peer_out_call(idx, w2, x1, mod, ln2_w, ln2_b, tab, alpha, tb, t_len, ctx_len):
    n, d = x1.shape
    seg = lambda i: (((i * tb) % t_len) >= ctx_len).astype(jnp.int32)
    return pl.pallas_call(
        functools.partial(_peer_out_kernel, alpha=alpha),
        grid=(n // tb,),
        in_specs=[pl.BlockSpec((tb * NSEL,), lambda i: (i,), memory_space=pltpu.SMEM),
                  pl.BlockSpec((tb, 2 * NSEL), lambda i: (i, 0)),
                  pl.BlockSpec((tb, d), lambda i: (i, 0)),
                  pl.BlockSpec((1, 1, 6, d), lambda i: ((i * tb) // t_len, seg(i), 0, 0)),
                  pl.BlockSpec((1, d), lambda i: (0, 0)),
                  pl.BlockSpec((1, d), lambda i: (0, 0)),
                  pl.BlockSpec(memory_space=pltpu.VMEM)],
        out_specs=pl.BlockSpec((tb, d), lambda i: (i, 0)),
        out_shape=jax.ShapeDtypeStruct((n, d), F32),
        scratch_shapes=[pltpu.VMEM((OUT_GROUP, PACK_ROWS * NSEL, LANES), jnp.int32),
                        pltpu.VMEM((tb, d), F32)],
        compiler_params=_cparams(1),
        name="peer_out",
    )(idx, w2, x1, mod, ln2_w.reshape(1, d), ln2_b.reshape(1, d), tab)


def _num_parts(n_tok):
    return next(p for p in range(PEER_PARTS, 0, -1) if n_tok % (p * ROUTE_TB) == 0)


def _peer_ffn(qp, u2, x1, mod, subkeys, peer_u, peer_v, ln2_w, ln2_b, alpha, t_len, ctx_len):
    n_tok, d = u2.shape
    parts = _num_parts(n_tok)
    n_part = n_tok // parts
    tab_u = _pack_table(peer_u)
    e_t, gate_t = _peer_route_call(qp, subkeys, ROUTE_TB, n_part)
    idx_parts, w2_parts = [], []
    for p in range(parts):
        idx_p = (jnp.swapaxes(e_t, 1, 2).reshape(n_part * NSEL) * PACK_ROWS).astype(jnp.int32)
        gate2_p = jnp.repeat(jnp.swapaxes(gate_t, 1, 2).reshape(n_part, NSEL), 2, axis=1)
        idx_parts.append(idx_p)
        if p + 1 < parts:
            w2_p, e_t, gate_t = _peer_act_call(idx_p, u2, gate2_p, tab_u, p * n_part, n_part,
                                               route=(qp, subkeys, (p + 1) * n_part))
        else:
            w2_p, = _peer_act_call(idx_p, u2, gate2_p, tab_u, p * n_part, n_part)
        w2_parts.append(w2_p)
    idx = jnp.concatenate(idx_parts)
    w2 = jnp.concatenate(w2_parts, axis=0)
    return _peer_out_call(idx, w2, x1, mod, ln2_w, ln2_b, _pack_table(peer_v), alpha, ROUTE_TB, t_len, ctx_len)


def _rope_tables(n_rows, ctx_len):
    row = jnp.repeat(jnp.arange(n_rows), GRID_W).astype(F32)
    col = jnp.tile(jnp.arange(GRID_W), n_rows).astype(F32)
    quarter = HEAD_DIM // 4
    inv_freq = ROPE_BASE ** (-2.0 * jnp.arange(quarter, dtype=F32) / (HEAD_DIM // 2))
    ang_r = row[:, None] * inv_freq
    ang_c = col[:, None] * inv_freq
    cr, sr, cc, sc = jnp.cos(ang_r), jnp.sin(ang_r), jnp.cos(ang_c), jnp.sin(ang_c)
    cos = jnp.concatenate([cr, cr, cc, cc], axis=-1)
    sin = jnp.concatenate([-sr, sr, -sc, sc], axis=-1)
    cos = jnp.concatenate([jnp.ones((ctx_len, HEAD_DIM), F32), cos], axis=0)
    sin = jnp.concatenate([jnp.zeros((ctx_len, HEAD_DIM), F32), sin], axis=0)
    reps = QK_COLS // HEAD_DIM
    return jnp.tile(cos, (1, reps)), jnp.tile(sin, (1, reps))


def _swap_cols():
    q = HEAD_DIM // 4
    one = np.concatenate([np.arange(q, 2 * q), np.arange(0, q), np.arange(3 * q, 4 * q), np.arange(2 * q, 3 * q)])
    return np.concatenate([one + HEAD_DIM * i for i in range(QK_COLS // HEAD_DIM)])


def kernel(x, c, ctx, c_ctx, ada_w, ada_b, w_in, rwkv_conv, rwkv_w0, rwkv_w2, rwkv_a0, rwkv_a2, rwkv_g2, rwkv_k_k, rwkv_k_a, rwkv_r_k, rwkv_ln_w, rwkv_ln_b, diff_lambda, diff_subln_w, hgrn_lb_logits, hgrn_norm_w, w_out, ln1_w, ln1_b, peer_wq, peer_subkeys, peer_u, peer_v, ln2_w, ln2_b):
    bsz, seq, d = x.shape
    ctx_len = ctx.shape[1]
    depth = w_in.shape[0]
    t_len = ctx_len + seq
    tm = 256 if ctx_len % 256 == 0 else 128
    assert ctx_len % tm == 0 and seq % tm == 0 and seq % GRID_W == 0
    ctx_blocks = ctx_len // tm
    alpha = (2.0 * depth) ** 0.25

    consts = _scan_consts()
    hsum = jnp.asarray((np.arange(A_WIDTH)[:, None] // HEAD_DIM == np.arange(A_WIDTH)[None, :] // HEAD_DIM)
                       .astype(np.float32))
    cos, sin = _rope_tables(seq // GRID_W, ctx_len)
    swap = _swap_cols()

    lb_p = jax.nn.softmax(hgrn_lb_logits.astype(F32), axis=0)
    lower_bounds = jnp.cumsum(lb_p, axis=0) - lb_p[0]

    n_cond = 8 * ((bsz + 1 + 7) // 8)
    cc = jnp.zeros((n_cond, d), F32).at[0].set(c_ctx).at[1:1 + bsz].set(c)
    mods = _ada_call(cc, ada_w, ada_b)

    xs = jnp.concatenate([ctx, x], axis=1)
    for l in range(depth):
        last = l == depth - 1
        m = mods[l].reshape(n_cond, 6, d)
        mod = jnp.stack([jnp.broadcast_to(m[0], (bsz, 6, d)), m[1:1 + bsz]], axis=1)
        q_cols = w_in[l][:, A_COLS:A_COLS + QK_COLS]
        k_cols = w_in[l][:, A_COLS + QK_COLS:A_COLS + 2 * QK_COLS]
        w_ext = jnp.concatenate([w_in[l], q_cols[:, swap], k_cols[:, swap]], axis=1).astype(BF16)
        P = dict(rwkv_conv=rwkv_conv[l], rwkv_w0=rwkv_w0[l], rwkv_w2=rwkv_w2[l], rwkv_a0=rwkv_a0[l],
                 rwkv_a2=rwkv_a2[l], rwkv_g2=rwkv_g2[l], rwkv_k_k=rwkv_k_k[l], rwkv_k_a=rwkv_k_a[l],
                 rwkv_r_k=rwkv_r_k[l], rwkv_ln_w=rwkv_ln_w[l], rwkv_ln_b=rwkv_ln_b[l],
                 hgrn_norm_w=hgrn_norm_w[l], w_out_bf16=w_out[l].astype(BF16),
                 ln1_w=ln1_w[l], ln1_b=ln1_b[l], peer_wq_bf16=peer_wq[l].astype(BF16))

        oa, q, k, v, oc = _inproj_call(xs, mod, w_ext, cos, sin, tm, ctx_blocks)
        r, k2, vv, avec, bvec, lw, g, bonus = _rwkv_prep_call(oa, P, hsum, tm, ctx_blocks)
        rw_o = _rwkv_scan_call(r, k2, vv, avec, bvec, lw, consts, ctx_len // CHUNK)
        hg_o = _hgrn_scan_call(oc, lower_bounds[l], consts, ctx_len // CHUNK)
        lam_init = 0.8 - 0.6 * math.exp(-0.3 * l)
        row0 = ctx_blocks if last else 0
        t_out = t_len - row0 * tm
        yb = _attn_call(q, k, v, diff_lambda[l], diff_subln_w[l], lam_init, tm, ctx_len, skip_ctx=last)
        x1, u2, qp = _outproj_call(xs, mod, rw_o, g, bonus, yb, hg_o, oc, hsum, P, alpha, tm, ctx_blocks, row0)

        x2 = _peer_ffn(qp.reshape(bsz * t_out, -1), u2.reshape(bsz * t_out, d), x1.reshape(bsz * t_out, d), mod,
                       peer_subkeys[l], peer_u[l], peer_v[l], ln2_w[l], ln2_b[l], alpha, t_out,
                       ctx_len - row0 * tm)
        xs = x2.reshape(bsz, t_out, d)
    return xs
```
